```python
import math
import jax, jax.numpy as jnp
from jax import lax
import numpy as np

D_MODEL = 1024
BATCH = 16
SEQ = 2048
DEPTH = 1

MEM_LEN = 256
MIX_WIDTH = D_MODEL
ML_HEADS = 4
ML_WIDTH = MIX_WIDTH // 2
ML_HEAD_DIM = ML_WIDTH // ML_HEADS
ML_CONV = 4
ML_CHUNK = 64
DSA_HEADS = 8
DSA_WIDTH = MIX_WIDTH - ML_WIDTH
DSA_HEAD_DIM = DSA_WIDTH // DSA_HEADS
DSA_LATENT = D_MODEL // 8
IDX_HEADS = 8
IDX_DIM = 64
INDEX_TOPK = 256
Q_BLOCK = 128
XA_HEADS = 4
XA_HEAD_DIM = D_MODEL // XA_HEADS
D_FF = ((8 * D_MODEL // 3 + 127) // 128) * 128
EPS = 1e-6

IN_SPLITS = (ML_WIDTH, ML_WIDTH, ML_WIDTH, ML_HEADS, ML_HEADS, ML_WIDTH,
             DSA_HEADS * DSA_LATENT, DSA_LATENT, IDX_HEADS * IDX_DIM, IDX_DIM, IDX_HEADS)
D_IN = sum(IN_SPLITS)

kernel_name = "hymba_mlstm_dsa_macaron"

F32 = jnp.float32


def rmsnorm(x, g):
    xf = x.astype(F32)
    y = xf * lax.rsqrt(jnp.mean(xf * xf, axis=-1, keepdims=True) + EPS)
    return (y * g.astype(F32)).astype(x.dtype)


def swiglu(x, w_gate, w_up, w_down):
    return (jax.nn.silu(x @ w_gate) * (x @ w_up)) @ w_down


def causal_dwconv(x, w, b):
    k = w.shape[0]
    y = lax.conv_general_dilated(x, w[:, None, :].astype(x.dtype), window_strides=(1,),
                                 padding=[(k - 1, 0)],
                                 dimension_numbers=('NWC', 'WIO', 'NWC'),
                                 feature_group_count=x.shape[-1])
    return y + b.astype(x.dtype)


def mlstm_chunkwise(q, k, v, ig, lf):
    B, H, S, d = q.shape
    L = ML_CHUNK
    nc = S // L
    ch = lambda t: jnp.moveaxis(t.reshape(B, H, nc, L, *t.shape[3:]), 2, 0)
    causal = jnp.tril(jnp.ones((L, L), dtype=bool))

    def step(carry, inp):
        C, n, m = carry
        qc, kc, vc, ic, fc = inp
        b = jnp.cumsum(fc, axis=-1)
        logw = jnp.where(causal, b[..., :, None] - b[..., None, :] + ic[..., None, :], -jnp.inf)
        inter = b + m[..., None]
        mj = jnp.maximum(inter, jnp.max(logw, axis=-1))
        w = jnp.exp(logw - mj[..., None])
        a = jnp.exp(inter - mj)
        sqk = jnp.einsum('bhjd,bhsd->bhjs', qc, kc) * w
        num = a[..., None] * jnp.einsum('bhed,bhjd->bhje', C, qc) + jnp.einsum('bhjs,bhse->bhje', sqk, vc)
        den = a * jnp.einsum('bhd,bhjd->bhj', n, qc) + jnp.sum(sqk, axis=-1)
        h = num / jnp.maximum(jnp.abs(den), jnp.exp(-mj))[..., None]
        bL = b[..., -1]
        g = bL[..., None] - b + ic
        m_new = jnp.maximum(bL + m, jnp.max(g, axis=-1))
        wg = jnp.exp(g - m_new[..., None])
        dec = jnp.exp(bL + m - m_new)
        C = dec[..., None, None] * C + jnp.einsum('bhs,bhse,bhsd->bhed', wg, vc, kc)
        n = dec[..., None] * n + jnp.einsum('bhs,bhsd->bhd', wg, kc)
        return (C, n, m_new), h

    init = (jnp.zeros((B, H, d, d), F32), jnp.zeros((B, H, d), F32), jnp.zeros((B, H), F32))
    _, hs = lax.scan(step, init, (ch(q), ch(k), ch(v), ch(ig), ch(lf)))
    return jnp.moveaxis(hs, 0, 2).reshape(B, H, S, d)


def mlstm_group(q, k, v, i_pre, f_pre, o_pre, head_g):
    B, S, _ = q.shape
    heads = lambda t: t.reshape(B, S, ML_HEADS, ML_HEAD_DIM).transpose(0, 2, 1, 3).astype(F32)
    qh = heads(q)
    kh = heads(k) * (ML_HEAD_DIM ** -0.5)
    vh = heads(v)
    ig = i_pre.astype(F32).transpose(0, 2, 1)
    lf = jax.nn.log_sigmoid(f_pre.astype(F32)).transpose(0, 2, 1)
    h = mlstm_chunkwise(qh, kh, vh, ig, lf).transpose(0, 2, 1, 3)
    h = rmsnorm(h, head_g.reshape(ML_HEADS, ML_HEAD_DIM))
    h = h.reshape(B, S, ML_WIDTH) * jax.nn.sigmoid(o_pre.astype(F32))
    return h.astype(q.dtype)


def dsa_group(dq, dc, iq, ik, iw, kv_g, idx_g, w_uv):
    B, S, _ = dq.shape
    qm = dq.reshape(B, S, DSA_HEADS, DSA_LATENT)
    ckv = rmsnorm(dc, kv_g)
    qi = iq.reshape(B, S, IDX_HEADS, IDX_DIM)
    ki = rmsnorm(ik, idx_g)
    wi = iw.astype(F32) * (IDX_HEADS ** -0.5)
    topk = min(INDEX_TOPK, S // 4)
    nb = S // Q_BLOCK
    blk = lambda t: jnp.moveaxis(t.reshape(B, nb, Q_BLOCK, *t.shape[2:]), 1, 0)
    key_pos = jnp.arange(S)

    def one_block(args):
        bi, qm_b, qi_b, wi_b = args
        qpos = bi * Q_BLOCK + jnp.arange(Q_BLOCK)
        sc = jnp.einsum('bqhd,bsd->bqhs', qi_b, ki).astype(F32) * (IDX_DIM ** -0.5)
        score = jnp.einsum('bqh,bqhs->bqs', wi_b, jax.nn.relu(sc))
        causal = key_pos[None, :] <= qpos[:, None]
        score = jnp.where(causal[None], score, -jnp.inf)
        _, idx = lax.top_k(score, topk)
        valid = idx <= qpos[None, :, None]
        kv = jax.vmap(lambda c, i: c[i])(ckv, idx)
        lg = jnp.einsum('bqhc,bqkc->bqhk', qm_b, kv).astype(F32) * (DSA_LATENT ** -0.5)
        lg = jnp.where(valid[:, :, None, :], lg, -jnp.inf)
        p = jax.nn.softmax(lg, axis=-1).astype(kv.dtype)
        return jnp.einsum('bqhk,bqkc->bqhc', p, kv)

    o = lax.map(one_block, (jnp.arange(nb), blk(qm), blk(qi), blk(wi)))
    o = jnp.moveaxis(o, 0, 1).reshape(B, S, DSA_HEADS, DSA_LATENT)
    return jnp.einsum('bshc,hcv->bshv', o, w_uv).reshape(B, S, DSA_WIDTH)


def cross_attn(u, memn, w_q, w_kv, w_o):
    B, S, _ = u.shape
    M = memn.shape[1]
    q = (u @ w_q).reshape(B, S, XA_HEADS, XA_HEAD_DIM)
    k, v = jnp.split(memn @ w_kv, 2, axis=-1)
    k = k.reshape(B, M, XA_HEADS, XA_HEAD_DIM)
    v = v.reshape(B, M, XA_HEADS, XA_HEAD_DIM)
    lg = jnp.einsum('bshd,bmhd->bhsm', q, k).astype(F32) * (XA_HEAD_DIM ** -0.5)
    p = jax.nn.softmax(lg, axis=-1).astype(v.dtype)
    o = jnp.einsum('bhsm,bmhd->bshd', p, v).reshape(B, S, D_MODEL)
    return o @ w_o


def setup_inputs(seed: int = 0) -> dict:
    key = jax.random.key(seed)
    ks = jax.random.split(key, 32)
    nrm = lambda k, shape, s: jax.random.normal(k, shape, F32) * s
    gain = lambda k, shape: 1.0 + 0.02 * jax.random.normal(k, shape, F32)
    L = DEPTH
    return {
        "x": nrm(ks[0], (BATCH, SEQ, D_MODEL), 1.0),
        "mem": nrm(ks[1], (BATCH, MEM_LEN, D_MODEL), 1.0),
        "ffn1_norm_g": gain(ks[2], (L, D_MODEL)),
        "ffn1_w_gate": nrm(ks[3], (L, D_MODEL, D_FF), D_MODEL ** -0.5),
        "ffn1_w_up": nrm(ks[4], (L, D_MODEL, D_FF), D_MODEL ** -0.5),
        "ffn1_w_down": nrm(ks[5], (L, D_FF, D_MODEL), D_FF ** -0.5),
        "mix_norm_g": gain(ks[6], (L, D_MODEL)),
        "w_in": nrm(ks[7], (L, D_MODEL, D_IN), D_MODEL ** -0.5),
        "mlstm_conv_w": nrm(ks[8], (L, ML_CONV, 2 * ML_WIDTH), ML_CONV ** -0.5),
        "mlstm_conv_b": nrm(ks[9], (L, 2 * ML_WIDTH), 0.02),
        "mlstm_i_bias": nrm(ks[10], (L, ML_HEADS), 0.1),
        "mlstm_f_bias": jnp.linspace(3.0, 6.0, ML_HEADS, dtype=F32)[None, :] + nrm(ks[11], (L, ML_HEADS), 0.1),
        "mlstm_head_norm_g": gain(ks[12], (L, ML_WIDTH)),
        "dsa_kv_norm_g": gain(ks[13], (L, DSA_LATENT)),
        "idx_k_norm_g": gain(ks[14], (L, IDX_DIM)),
        "dsa_w_uv": nrm(ks[15], (L, DSA_HEADS, DSA_LATENT, DSA_HEAD_DIM), DSA_LATENT ** -0.5),
        "w_out": nrm(ks[16], (L, MIX_WIDTH, D_MODEL), MIX_WIDTH ** -0.5),
        "xattn_norm_g": gain(ks[17], (L, D_MODEL)),
        "mem_norm_g": gain(ks[18], (L, D_MODEL)),
        "xattn_w_q": nrm(ks[19], (L, D_MODEL, D_MODEL), D_MODEL ** -0.5),
        "xattn_w_kv": nrm(ks[20], (L, D_MODEL, 2 * D_MODEL), D_MODEL ** -0.5),
        "xattn_w_o": nrm(ks[21], (L, D_MODEL, D_MODEL), D_MODEL ** -0.5),
        "ffn2_norm_g": gain(ks[22], (L, D_MODEL)),
        "ffn2_w_gate": nrm(ks[23], (L, D_MODEL, D_FF), D_MODEL ** -0.5),
        "ffn2_w_up": nrm(ks[24], (L, D_MODEL, D_FF), D_MODEL ** -0.5),
        "ffn2_w_down": nrm(ks[25], (L, D_FF, D_MODEL), D_FF ** -0.5),
        "final_norm_g": gain(ks[26], (D_MODEL,)),
    }


def reference(x, mem, ffn1_norm_g, ffn1_w_gate, ffn1_w_up, ffn1_w_down, mix_norm_g, w_in,
              mlstm_conv_w, mlstm_conv_b, mlstm_i_bias, mlstm_f_bias, mlstm_head_norm_g,
              dsa_kv_norm_g, idx_k_norm_g, dsa_w_uv, w_out, xattn_norm_g, mem_norm_g,
              xattn_w_q, xattn_w_kv, xattn_w_o, ffn2_norm_g, ffn2_w_gate, ffn2_w_up,
              ffn2_w_down, final_norm_g):
    offsets = [int(c) for c in np.cumsum(IN_SPLITS)[:-1]]
    h = x
    for l in range(DEPTH):
        h = h + 0.5 * swiglu(rmsnorm(h, ffn1_norm_g[l]), ffn1_w_gate[l], ffn1_w_up[l], ffn1_w_down[l])
        u = rmsnorm(h, mix_norm_g[l])
        z = u @ w_in[l]
        mq, mk, mv, mi, mf, mo, dq, dc, iq, ik, iw = jnp.split(z, offsets, axis=-1)
        qk = jax.nn.silu(causal_dwconv(jnp.concatenate([mq, mk], axis=-1), mlstm_conv_w[l], mlstm_conv_b[l]))
        mq, mk = jnp.split(qk, 2, axis=-1)
        y_ml = mlstm_group(mq, mk, mv, mi + mlstm_i_bias[l], mf + mlstm_f_bias[l], mo, mlstm_head_norm_g[l])
        y_dsa = dsa_group(dq, dc, iq, ik, iw, dsa_kv_norm_g[l], idx_k_norm_g[l], dsa_w_uv[l])
        h = h + jnp.concatenate([y_ml, y_dsa], axis=-1) @ w_out[l]
        h = h + cross_attn(rmsnorm(h, xattn_norm_g[l]), rmsnorm(mem, mem_norm_g[l]),
                           xattn_w_q[l], xattn_w_kv[l], xattn_w_o[l])
        h = h + 0.5 * swiglu(rmsnorm(h, ffn2_norm_g[l]), ffn2_w_gate[l], ffn2_w_up[l], ffn2_w_down[l])
    return rmsnorm(h, final_norm_g)
```

```python
import functools

import jax
import jax.numpy as jnp
from jax import lax
from jax.experimental import pallas as pl
from jax.experimental.pallas import tpu as pltpu

F32 = jnp.float32
BF16 = jnp.bfloat16
I32 = jnp.int32
EPS = 1e-6

ML_HEADS = 4
ML_HEAD_DIM = 128
ML_WIDTH = ML_HEADS * ML_HEAD_DIM
ML_CONV = 4
DSA_HEADS = 8
DSA_LATENT = 128
DSA_HEAD_DIM = 64
DSA_WIDTH = DSA_HEADS * DSA_HEAD_DIM
IDX_HEADS = 8
IDX_DIM = 64
INDEX_TOPK = 256
XA_HEADS = 4

FFN_TM = 512
FFN_FC = 256
PROJ_TM = 512
ML_CHUNK = 128
DSA_TQ = 256
MIX_TM = 512

CKVT_ROWS = DSA_LATENT + 16

VMEM_LIMIT = 56 * 1024 * 1024
INT_MIN = -(2 ** 31)
NEG_INF = float("-inf")
LOG2E = 1.4426950408889634


def _rmsnorm(x, g):
    return x * lax.rsqrt(jnp.mean(x * x, axis=-1, keepdims=True) + EPS) * g


def _sigmoid(x):
    return 1.0 / (1.0 + jnp.exp(-x))


def _dot(a, b):
    return jnp.dot(a, b, preferred_element_type=F32)


def _dot_nt(a, b):
    return lax.dot_general(a, b, (((1,), (1,)), ((), ())), preferred_element_type=F32)


def _full(shape):
    return pl.BlockSpec(shape, lambda *_: (0,) * len(shape))


def _params(*sem):
    return pltpu.CompilerParams(dimension_semantics=sem, vmem_limit_bytes=VMEM_LIMIT)


def _ffn_kernel(x_ref, g_ref, wg_ref, wu_ref, wd_ref, fg_ref, o_ref, *, final_norm):
    x = x_ref[...]
    xn = _rmsnorm(x, g_ref[...]).astype(BF16)
    d_ff = wg_ref.shape[1]
    acc = jnp.zeros(x.shape, F32)
    for c in range(d_ff // FFN_FC):
        sl = slice(c * FFN_FC, (c + 1) * FFN_FC)
        gate = _dot(xn, wg_ref[:, sl])
        up = _dot(xn, wu_ref[:, sl])
        act = (gate * _sigmoid(gate) * up).astype(BF16)
        acc = acc + _dot(act, wd_ref[sl, :])
    y = x + 0.5 * acc
    if final_norm:
        y = _rmsnorm(y, fg_ref[...])
    o_ref[...] = y


def _ffn(x, g, wg, wu, wd, fg, final_norm):
    t, d = x.shape
    d_ff = wg.shape[1]
    return pl.pallas_call(
        functools.partial(_ffn_kernel, final_norm=final_norm),
        grid=(t // FFN_TM,),
        in_specs=[
            pl.BlockSpec((FFN_TM, d), lambda i: (i, 0)),
            _full((1, d)),
            _full((d, d_ff)),
            _full((d, d_ff)),
            _full((d_ff, d)),
            _full((1, d)),
        ],
        out_specs=pl.BlockSpec((FFN_TM, d), lambda i: (i, 0)),
        out_shape=jax.ShapeDtypeStruct((t, d), F32),
        compiler_params=_params("parallel"),
        name="ffn_final" if final_norm else "ffn",
    )(x, g, wg, wu, wd, fg)


def _proj_kernel(h_ref, g_ref, wqk_ref, wv_ref, wo_ref, wgc_ref, wgr_ref, bgc_ref, bgr_ref,
                 cw_ref, cb_ref, wdq_ref, wiq_ref, wiw_ref, wdc_ref, kvg_ref, wik_ref, ikg_ref,
                 q_ref, k_ref, v_ref, o_ref, gc_ref, gr_ref, dqt_ref, iqt_ref, wit_ref,
                 ckv_ref, ckvt_ref, ki_ref, zbuf, *, tiles_per_seq):
    tm = h_ref.shape[0]
    u = _rmsnorm(h_ref[...], g_ref[...]).astype(BF16)

    @pl.when(pl.program_id(0) % tiles_per_seq == 0)
    def _():
        zbuf[0:8, :] = jnp.zeros((8, zbuf.shape[1]), F32)

    zbuf[8:8 + tm, :] = _dot(u, wqk_ref[...])
    y = jnp.zeros((tm, zbuf.shape[1]), F32) + cb_ref[...]
    for j in range(ML_CONV):
        y = y + zbuf[5 + j:5 + j + tm, :] * cw_ref[j:j + 1, :]
    zbuf[0:8, :] = zbuf[tm:tm + 8, :]
    y = y * _sigmoid(y)
    q_ref[...] = y[:, :ML_WIDTH].astype(BF16)
    k_ref[...] = (y[:, ML_WIDTH:] * (ML_HEAD_DIM ** -0.5)).astype(BF16)
    v_ref[...] = _dot(u, wv_ref[...]).astype(BF16)
    o_ref[...] = _dot(u, wo_ref[...])

    def gates(z, idx):
        return jnp.where(idx < ML_HEADS, z, jnp.minimum(z, 0.0) - jnp.log(1.0 + jnp.exp(-jnp.abs(z))))

    zc = _dot(u, wgc_ref[...]) + bgc_ref[...]
    gc_ref[...] = gates(zc, lax.broadcasted_iota(I32, zc.shape, 1))
    zr = _dot_nt(wgr_ref[...], u) + bgr_ref[...]
    gr_ref[...] = gates(zr, lax.broadcasted_iota(I32, zr.shape, 0))

    dqt_ref[...] = (_dot_nt(wdq_ref[...], u) * (DSA_LATENT ** -0.5 * LOG2E)).astype(BF16)
    iqt_ref[...] = _dot_nt(wiq_ref[...], u).astype(BF16)
    wit_ref[...] = _dot_nt(wiw_ref[...], u) * (IDX_HEADS ** -0.5) * (IDX_DIM ** -0.5)
    ckv = _rmsnorm(_dot(u, wdc_ref[...]), kvg_ref[...])
    ckv_ref[...] = ckv.astype(BF16)
    ckvt_ref[0:DSA_LATENT, :] = ckv.T.astype(BF16)
    ckvt_ref[DSA_LATENT:, :] = (lax.broadcasted_iota(I32, (CKVT_ROWS - DSA_LATENT, tm), 0) == 0).astype(BF16)
    ik = _dot(u, wik_ref[...])[:, :IDX_DIM]
    ki_ref[...] = _rmsnorm(ik, ikg_ref[...]).astype(BF16)


def _proj(h1, g, w, seq):
    t, d = h1.shape
    tm = PROJ_TM
    row = lambda n: pl.BlockSpec((tm, n), lambda i: (i, 0))
    col = lambda n: pl.BlockSpec((n, tm), lambda i: (0, i))
    outs = [
        ("q", row(ML_WIDTH), (t, ML_WIDTH), BF16),
        ("k", row(ML_WIDTH), (t, ML_WIDTH), BF16),
        ("v", row(ML_WIDTH), (t, ML_WIDTH), BF16),
        ("o", row(ML_WIDTH), (t, ML_WIDTH), F32),
        ("gc", row(128), (t, 128), F32),
        ("gr", col(16), (16, t), F32),
        ("dqt", col(DSA_HEADS * DSA_LATENT), (DSA_HEADS * DSA_LATENT, t), BF16),
        ("iqt", col(IDX_HEADS * IDX_DIM), (IDX_HEADS * IDX_DIM, t), BF16),
        ("wit", col(16), (16, t), F32),
        ("ckv", row(DSA_LATENT), (t, DSA_LATENT), BF16),
        ("ckvt", col(CKVT_ROWS), (CKVT_ROWS, t), BF16),
        ("ki", row(IDX_DIM), (t, IDX_DIM), BF16),
    ]
    ins = [h1, g, w["wqk"], w["wv"], w["wo"], w["wgc"], w["wgr"], w["bgc"], w["bgr"], w["cw"], w["cb"],
           w["wdq_t"], w["wiq_t"], w["wiw_t"], w["wdc"], w["kvg"], w["wik"], w["ikg"]]
    res = pl.pallas_call(
        functools.partial(_proj_kernel, tiles_per_seq=seq // tm),
        grid=(t // tm,),
        in_specs=[row(d)] + [_full(a.shape) for a in ins[1:]],
        out_specs=[o[1] for o in outs],
        out_shape=[jax.ShapeDtypeStruct(o[2], o[3]) for o in outs],
        scratch_shapes=[pltpu.VMEM((tm + 8, 2 * ML_WIDTH), F32)],
        compiler_params=_params("arbitrary"),
        name="proj",
    )(*ins)
    return {o[0]: r for o, r in zip(outs, res)}


def _split3(x):
    hi = x.astype(BF16)
    r1 = x - hi.astype(F32)
    mid = r1.astype(BF16)
    lo = (r1 - mid.astype(F32)).astype(BF16)
    return hi, mid, lo


def _mlstm_kernel(q_ref, k_ref, v_ref, o_ref, gc_ref, gr_ref, hg_ref, y_ref, c_ref, m_ref):
    seq = q_ref.shape[0]
    L = ML_CHUNK
    d = ML_HEAD_DIM
    c_ref[...] = jnp.zeros(c_ref.shape, F32)
    m_ref[...] = jnp.zeros(m_ref.shape, F32)

    ri = lax.broadcasted_iota(I32, (L, L), 0)
    ci = lax.broadcasted_iota(I32, (L, L), 1)
    causal = ci <= ri
    tri_lo = causal.astype(BF16)
    tri_up = (ri <= ci).astype(BF16)
    ones_col = (lax.broadcasted_iota(I32, (L, d), 1) == 0).astype(F32)

    def chunk(c, carry):
        r0 = pl.multiple_of(c * L, L)
        gc = gc_ref[pl.ds(r0, L), :]
        gr = gr_ref[:, pl.ds(r0, L)]
        bc = sum(_dot(tri_lo, p) for p in _split3(gc))
        br = sum(_dot(p, tri_up) for p in _split3(gr))
        for h in range(ML_HEADS):
            hs = slice(h * d, (h + 1) * d)
            qh = q_ref[pl.ds(r0, L), hs]
            kh = k_ref[pl.ds(r0, L), hs]
            vh = v_ref[pl.ds(r0, L), hs].astype(F32)
            m_prev = m_ref[h:h + 1, 0:1]
            b_col = bc[:, ML_HEADS + h:ML_HEADS + h + 1]
            b_row = br[ML_HEADS + h:ML_HEADS + h + 1, :]
            i_col = gc[:, h:h + 1]
            i_row = gr[h:h + 1, :]
            logw = jnp.where(causal, b_col - b_row + i_row, NEG_INF)
            inter = b_col + m_prev
            mj = jnp.maximum(inter, jnp.max(logw, axis=1, keepdims=True))
            w = jnp.exp(logw - mj)
            a = jnp.exp(inter - mj)
            sqk = _dot_nt(qh, kh) * w
            ct = c_ref[h]
            v_aug = jnp.concatenate([vh, ones_col], axis=1)
            tot = a * _dot(qh, ct.astype(BF16)) + _dot(sqk.astype(BF16), v_aug.astype(BF16))
            num = tot[:, :d]
            den = tot[:, d:d + 1]
            hh = num / jnp.maximum(jnp.abs(den), jnp.exp(-mj))
            hn = _rmsnorm(hh, hg_ref[:, hs])
            y_ref[pl.ds(r0, L), hs] = (hn * _sigmoid(o_ref[pl.ds(r0, L), hs])).astype(BF16)
            b_last = b_row[:, L - 1:L]
            g_col = b_last - b_col + i_col
            g_row = b_last - b_row + i_row
            m_new = jnp.maximum(b_last + m_prev, jnp.max(g_row, axis=1, keepdims=True))
            wg = jnp.exp(g_col - m_new)
            dec = jnp.exp(b_last + m_prev - m_new)
            kt = kh.astype(F32).T.astype(BF16)
            c_ref[h] = dec * ct + _dot(kt, (wg * v_aug).astype(BF16))
            m_ref[h:h + 1, :] = jnp.broadcast_to(m_new, (1, m_ref.shape[1]))
        return carry

    lax.fori_loop(0, seq // L, chunk, 0)


def _mlstm(p, head_g, batch, seq):
    t = batch * seq
    row = lambda n: pl.BlockSpec((seq, n), lambda b: (b, 0))
    return pl.pallas_call(
        _mlstm_kernel,
        grid=(batch,),
        in_specs=[row(ML_WIDTH), row(ML_WIDTH), row(ML_WIDTH), row(ML_WIDTH), row(128),
                  pl.BlockSpec((16, seq), lambda b: (0, b)), _full((1, ML_WIDTH))],
        out_specs=row(ML_WIDTH),
        out_shape=jax.ShapeDtypeStruct((t, ML_WIDTH), BF16),
        scratch_shapes=[pltpu.VMEM((ML_HEADS, ML_HEAD_DIM, 2 * ML_HEAD_DIM), F32),
                        pltpu.VMEM((8, 128), F32)],
        compiler_params=_params("parallel"),
        name="mlstm",
    )(p["q"], p["k"], p["v"], p["o"], p["gc"], p["gr"], head_g)


def _float_of_code(code):
    key = code ^ jnp.int32(INT_MIN)
    bits = jnp.where(key < 0, key ^ jnp.int32(0x7FFFFFFF), key)
    return lax.bitcast_convert_type(bits, F32)


def _dsa_block(nk, iqt_ref, wit_ref, dqt_ref, ki_ref, ckv_ref, ckvt_ref, wuvt_ref, y_ref,
               score_ref, bias_ref, yt_ref, *, topk):
    tq = iqt_ref.shape[1]
    kv = nk * tq
    d0 = kv - tq
    tri = lax.broadcasted_iota(I32, (tq, tq), 0) <= lax.broadcasted_iota(I32, (tq, tq), 1)

    ki = ki_ref[0:kv, :]
    score = jnp.zeros((kv, tq), F32)
    for h in range(IDX_HEADS):
        sc = _dot(ki, iqt_ref[h * IDX_DIM:(h + 1) * IDX_DIM, :])
        score = score + wit_ref[h:h + 1, :] * jnp.maximum(sc, 0.0)
    score_ref[0:kv, :] = score
    score_ref[d0:kv, :] = jnp.where(tri, score_ref[d0:kv, :], NEG_INF)

    def bit_pass(i, code):
        cand_code = code | lax.shift_left(jnp.int32(1), 31 - i)
        cand = _float_of_code(cand_code)
        cnt = jnp.sum(jnp.where(score_ref[0:kv, :] < cand, 0.0, 1.0), axis=0, keepdims=True)
        return jnp.where(cnt >= topk, cand_code, code)

    thr = _float_of_code(lax.fori_loop(0, 32, bit_pass, jnp.zeros((1, tq), I32)))
    score = score_ref[0:kv, :]
    n_gt = jnp.sum(jnp.where(score > thr, 1.0, 0.0), axis=0, keepdims=True)
    n_eq = jnp.sum(jnp.where(score == thr, 1.0, 0.0), axis=0, keepdims=True)
    room = topk - n_gt
    bias_ref[0:kv, :] = jnp.where(score >= thr, 0.0, NEG_INF)

    @pl.when(jnp.max(n_eq - room) > 0.0)
    def _():
        lo = (lax.broadcasted_iota(I32, (tq, tq), 1) < lax.broadcasted_iota(I32, (tq, tq), 0)).astype(BF16)
        before = jnp.zeros((1, tq), F32)
        for c in range(nk):
            rs = slice(c * tq, (c + 1) * tq)
            sb = score_ref[rs, :]
            ef = jnp.where(sb == thr, 1.0, 0.0)
            rank = before + _dot(lo, ef.astype(BF16))
            keep = (sb > thr) | ((sb == thr) & (rank < room))
            bias_ref[rs, :] = jnp.where(keep, 0.0, NEG_INF)
            before = before + jnp.sum(ef, axis=0, keepdims=True)

    bias_ref[d0:kv, :] = jnp.where(tri, bias_ref[d0:kv, :], NEG_INF)

    for h in range(DSA_HEADS):
        lg = _dot(ckv_ref[0:kv, :], dqt_ref[h * DSA_LATENT:(h + 1) * DSA_LATENT, :]) + bias_ref[0:kv, :]
        p = jnp.exp2(lg - jnp.max(lg, axis=0, keepdims=True)).astype(BF16)
        oa = _dot(ckvt_ref[:, 0:kv], p)
        ot = oa[0:DSA_LATENT, :] / oa[DSA_LATENT:DSA_LATENT + 1, :]
        yt_ref[h * DSA_HEAD_DIM:(h + 1) * DSA_HEAD_DIM, :] = _dot(wuvt_ref[h], ot.astype(BF16))
    y_ref[...] = yt_ref[...].T.astype(BF16)


def _dsa_kernel(*refs, topk, nq):
    for nk in range(1, nq + 1):
        pl.when(pl.program_id(1) == nk - 1)(functools.partial(_dsa_block, nk, *refs, topk=topk))


def _dsa(p, wuv_t, batch, seq):
    t = batch * seq
    tq = DSA_TQ
    nq = seq // tq
    topk = min(INDEX_TOPK, seq // 4)
    qcol = lambda n: pl.BlockSpec((n, tq), lambda b, i: (0, b * nq + i))
    return pl.pallas_call(
        functools.partial(_dsa_kernel, topk=float(topk), nq=nq),
        grid=(batch, nq),
        in_specs=[qcol(IDX_HEADS * IDX_DIM), qcol(16), qcol(DSA_HEADS * DSA_LATENT),
                  pl.BlockSpec((seq, IDX_DIM), lambda b, i: (b, 0)),
                  pl.BlockSpec((seq, DSA_LATENT), lambda b, i: (b, 0)),
                  pl.BlockSpec((CKVT_ROWS, seq), lambda b, i: (0, b)),
                  _full(wuv_t.shape)],
        out_specs=pl.BlockSpec((tq, DSA_WIDTH), lambda b, i: (b * nq + i, 0)),
        out_shape=jax.ShapeDtypeStruct((t, DSA_WIDTH), BF16),
        scratch_shapes=[pltpu.VMEM((seq, tq), F32), pltpu.VMEM((seq, tq), F32), pltpu.VMEM((DSA_WIDTH, tq), F32)],
        compiler_params=_params("parallel", "arbitrary"),
        name="dsa",
    )(p["iqt"], p["wit"], p["dqt"], p["ki"], p["ckv"], p["ckvt"], wuv_t)


def _memkv_kernel(m_ref, g_ref, w_ref, o_ref):
    o_ref[...] = _dot(_rmsnorm(m_ref[...], g_ref[...]).astype(BF16), w_ref[...]).astype(BF16)


def _memkv(mem2d, g, w_kv, mem_len):
    rows, d = mem2d.shape
    return pl.pallas_call(
        _memkv_kernel,
        grid=(rows // mem_len,),
        in_specs=[pl.BlockSpec((mem_len, d), lambda b: (b, 0)), _full((1, d)), _full(w_kv.shape)],
        out_specs=pl.BlockSpec((mem_len, w_kv.shape[1]), lambda b: (b, 0)),
        out_shape=jax.ShapeDtypeStruct((rows, w_kv.shape[1]), BF16),
        compiler_params=_params("parallel"),
        name="memkv",
    )(mem2d, g, w_kv)


def _mixout_kernel(h_ref, yml_ref, ydsa_ref, woa_ref, wob_ref, g_ref, wq_ref, kv_ref, wo_ref, o_ref):
    d = h_ref.shape[1]
    hd = d // XA_HEADS
    h2 = h_ref[...] + _dot(yml_ref[...], woa_ref[...]) + _dot(ydsa_ref[...], wob_ref[...])
    q = _dot(_rmsnorm(h2, g_ref[...]).astype(BF16), wq_ref[...]).astype(BF16)
    heads = []
    for h in range(XA_HEADS):
        kh = kv_ref[:, h * hd:(h + 1) * hd]
        vh = kv_ref[:, d + h * hd:d + (h + 1) * hd]
        lg = _dot_nt(q[:, h * hd:(h + 1) * hd], kh) * (hd ** -0.5)
        p = jnp.exp(lg - jnp.max(lg, axis=1, keepdims=True))
        denom = jnp.sum(p, axis=1, keepdims=True)
        heads.append((_dot(p.astype(BF16), vh) / denom).astype(BF16))
    o_ref[...] = h2 + _dot(jnp.concatenate(heads, axis=1), wo_ref[...])


def _mixout(h1, yml, ydsa, w, kv, batch, seq, mem_len):
    t, d = h1.shape
    tm = MIX_TM
    nt = seq // tm
    row = lambda n: pl.BlockSpec((tm, n), lambda b, j: (b * nt + j, 0))
    return pl.pallas_call(
        _mixout_kernel,
        grid=(batch, nt),
        in_specs=[row(d), row(ML_WIDTH), row(DSA_WIDTH), _full(w["woa"].shape), _full(w["wob"].shape),
                  _full((1, d)), _full(w["wq"].shape),
                  pl.BlockSpec((mem_len, 2 * d), lambda b, j: (b, 0)), _full(w["wxo"].shape)],
        out_specs=row(d),
        out_shape=jax.ShapeDtypeStruct((t, d), F32),
        compiler_params=_params("parallel", "parallel"),
        name="mixout",
    )(h1, yml, ydsa, w["woa"], w["wob"], w["xg"], w["wq"], kv, w["wxo"])


def _layer(h, mem2d, p, batch, seq, mem_len, final_g):
    h1 = _ffn(h, p["f1g"], p["f1wg"], p["f1wu"], p["f1wd"], p["f1g"], final_norm=False)
    pr = _proj(h1, p["mixg"], p, seq)
    yml = _mlstm(pr, p["headg"], batch, seq)
    ydsa = _dsa(pr, p["wuv_t"], batch, seq)
    kv = _memkv(mem2d, p["memg"], p["wkv"], mem_len)
    h3 = _mixout(h1, yml, ydsa, p, kv, batch, seq, mem_len)
    fg = p["f2g"] if final_g is None else final_g
    return _ffn(h3, p["f2g"], p["f2wg"], p["f2wu"], p["f2wd"], fg, final_norm=final_g is not None)


def kernel(x, mem, ffn1_norm_g, ffn1_w_gate, ffn1_w_up, ffn1_w_down, mix_norm_g, w_in, mlstm_conv_w, mlstm_conv_b, mlstm_i_bias, mlstm_f_bias, mlstm_head_norm_g, dsa_kv_norm_g, idx_k_norm_g, dsa_w_uv, w_out, xattn_norm_g, mem_norm_g, xattn_w_q, xattn_w_kv, xattn_w_o, ffn2_norm_g, ffn2_w_gate, ffn2_w_up, ffn2_w_down, final_norm_g):
    batch, seq, d = x.shape
    mem_len = mem.shape[1]
    depth = w_in.shape[0]
    h = x.reshape(batch * seq, d)
    mem2d = mem.reshape(batch * mem_len, d)
    row = lambda a: a.reshape(1, -1).astype(F32)
    b16 = lambda a: a.astype(BF16)

    splits = (ML_WIDTH, ML_WIDTH, ML_WIDTH, ML_HEADS, ML_HEADS, ML_WIDTH, DSA_HEADS * DSA_LATENT, DSA_LATENT,
              IDX_HEADS * IDX_DIM, IDX_DIM, IDX_HEADS)
    offs = [0]
    for s in splits:
        offs.append(offs[-1] + s)

    for l in range(depth):
        wi = w_in[l]
        cols = [wi[:, offs[i]:offs[i + 1]] for i in range(len(splits))]
        mq, mk, mv, mi, mf, mo, dq, dc, iq, ik, iw = cols
        w_gate = jnp.concatenate([mi, mf], axis=1)
        b_gate = jnp.concatenate([mlstm_i_bias[l], mlstm_f_bias[l]])
        p = {
            "f1g": row(ffn1_norm_g[l]), "f1wg": b16(ffn1_w_gate[l]), "f1wu": b16(ffn1_w_up[l]), "f1wd": b16(ffn1_w_down[l]),
            "f2g": row(ffn2_norm_g[l]), "f2wg": b16(ffn2_w_gate[l]), "f2wu": b16(ffn2_w_up[l]), "f2wd": b16(ffn2_w_down[l]),
            "mixg": row(mix_norm_g[l]),
            "wqk": b16(jnp.concatenate([mq, mk], axis=1)), "wv": b16(mv), "wo": b16(mo),
            "wgc": b16(jnp.pad(w_gate, ((0, 0), (0, 128 - 2 * ML_HEADS)))),
            "wgr": b16(jnp.pad(w_gate.T, ((0, 16 - 2 * ML_HEADS), (0, 0)))),
            "bgc": jnp.pad(b_gate, (0, 128 - 2 * ML_HEADS)).reshape(1, 128).astype(F32),
            "bgr": jnp.pad(b_gate, (0, 16 - 2 * ML_HEADS)).reshape(16, 1).astype(F32),
            "cw": mlstm_conv_w[l].astype(F32), "cb": row(mlstm_conv_b[l]),
            "wdq_t": b16(dq.T), "wiq_t": b16(iq.T), "wiw_t": b16(jnp.pad(iw.T, ((0, 16 - IDX_HEADS), (0, 0)))),
            "wdc": b16(dc), "kvg": row(dsa_kv_norm_g[l]),
            "wik": b16(jnp.pad(ik, ((0, 0), (0, 128 - IDX_DIM)))), "ikg": row(idx_k_norm_g[l]),
            "headg": row(mlstm_head_norm_g[l]),
            "wuv_t": b16(jnp.swapaxes(dsa_w_uv[l], 1, 2)),
            "woa": b16(w_out[l][:ML_WIDTH]), "wob": b16(w_out[l][ML_WIDTH:]),
            "xg": row(xattn_norm_g[l]), "memg": row(mem_norm_g[l]),
            "wq": b16(xattn_w_q[l]), "wkv": b16(xattn_w_kv[l]), "wxo": b16(xattn_w_o[l]),
        }
        fg = row(final_norm_g) if l == depth - 1 else None
        h = _layer(h, mem2d, p, batch, seq, mem_len, fg)
    return h.reshape(batch, seq, d)
```

```python
import functools

import jax
import jax.numpy as jnp
from jax import lax
from jax.experimental import pallas as pl
from jax.experimental.pallas import tpu as pltpu

F32 = jnp.float32
BF16 = jnp.bfloat16
I32 = jnp.int32
EPS = 1e-6

ML_HEADS = 4
ML_HEAD_DIM = 128
ML_WIDTH = ML_HEADS * ML_HEAD_DIM
ML_CONV = 4
DSA_HEADS = 8
DSA_LATENT = 128
DSA_HEAD_DIM = 64
DSA_WIDTH = DSA_HEADS * DSA_HEAD_DIM
IDX_HEADS = 8
IDX_DIM = 64
INDEX_TOPK = 256
XA_HEADS = 4

FFN_TM = 512
FFN_FC = 256
PROJ_TM = 512
ML_CHUNK = 128
DSA_TQ = 256
MIX_TM = 512

CKVT_ROWS = DSA_LATENT + 16

VMEM_LIMIT = 56 * 1024 * 1024
INT_MIN = -(2 ** 31)
NEG_INF = float("-inf")
LOG2E = 1.4426950408889634


def _rmsnorm(x, g):
    return x * lax.rsqrt(jnp.mean(x * x, axis=-1, keepdims=True) + EPS) * g


def _sigmoid(x):
    return 1.0 / (1.0 + jnp.exp(-x))


def _dot(a, b):
    return jnp.dot(a, b, preferred_element_type=F32)


def _dot_nt(a, b):
    return lax.dot_general(a, b, (((1,), (1,)), ((), ())), preferred_element_type=F32)


def _full(shape):
    return pl.BlockSpec(shape, lambda *_: (0,) * len(shape))


def _params(*sem):
    return pltpu.CompilerParams(dimension_semantics=sem, vmem_limit_bytes=VMEM_LIMIT)


def _ffn_kernel(x_ref, g_ref, wg_ref, wu_ref, wd_ref, fg_ref, o_ref, *, final_norm):
    x = x_ref[...]
    xn = _rmsnorm(x, g_ref[...]).astype(BF16)
    d_ff = wg_ref.shape[1]
    acc = jnp.zeros(x.shape, F32)
    for c in range(d_ff // FFN_FC):
        sl = slice(c * FFN_FC, (c + 1) * FFN_FC)
        gate = _dot(xn, wg_ref[:, sl])
        up = _dot(xn, wu_ref[:, sl])
        act = (gate * _sigmoid(gate) * up).astype(BF16)
        acc = acc + _dot(act, wd_ref[sl, :])
    y = x + 0.5 * acc
    if final_norm:
        y = _rmsnorm(y, fg_ref[...])
    o_ref[...] = y


def _ffn(x, g, wg, wu, wd, fg, final_norm):
    t, d = x.shape
    d_ff = wg.shape[1]
    return pl.pallas_call(
        functools.partial(_ffn_kernel, final_norm=final_norm),
        grid=(t // FFN_TM,),
        in_specs=[
            pl.BlockSpec((FFN_TM, d), lambda i: (i, 0)),
            _full((1, d)),
            _full((d, d_ff)),
            _full((d, d_ff)),
            _full((d_ff, d)),
            _full((1, d)),
        ],
        out_specs=pl.BlockSpec((FFN_TM, d), lambda i: (i, 0)),
        out_shape=jax.ShapeDtypeStruct((t, d), F32),
        compiler_params=_params("parallel"),
        name="ffn_final" if final_norm else "ffn",
    )(x, g, wg, wu, wd, fg)


def _proj_kernel(h_ref, g_ref, wqk_ref, wv_ref, wo_ref, wgc_ref, wgr_ref, bgc_ref, bgr_ref,
                 cw_ref, cb_ref, wdq_ref, wiq_ref, wiw_ref, wdc_ref, kvg_ref, wik_ref, ikg_ref,
                 q_ref, k_ref, v_ref, o_ref, gc_ref, gr_ref, dqt_ref, iqt_ref, wit_ref,
                 ckv_ref, ckvt_ref, ki_ref, zbuf, *, tiles_per_seq):
    tm = h_ref.shape[0]
    u = _rmsnorm(h_ref[...], g_ref[...]).astype(BF16)

    @pl.when(pl.program_id(0) % tiles_per_seq == 0)
    def _():
        zbuf[0:8, :] = jnp.zeros((8, zbuf.shape[1]), F32)

    zbuf[8:8 + tm, :] = _dot(u, wqk_ref[...])
    y = jnp.zeros((tm, zbuf.shape[1]), F32) + cb_ref[...]
    for j in range(ML_CONV):
        y = y + zbuf[5 + j:5 + j + tm, :] * cw_ref[j:j + 1, :]
    zbuf[0:8, :] = zbuf[tm:tm + 8, :]
    y = y * _sigmoid(y)
    q_ref[...] = y[:, :ML_WIDTH].astype(BF16)
    k_ref[...] = (y[:, ML_WIDTH:] * (ML_HEAD_DIM ** -0.5)).astype(BF16)
    v_ref[...] = _dot(u, wv_ref[...]).astype(BF16)
    o_ref[...] = _dot(u, wo_ref[...])

    def gates(z, idx):
        return jnp.where(idx < ML_HEADS, z, jnp.minimum(z, 0.0) - jnp.log(1.0 + jnp.exp(-jnp.abs(z))))

    zc = _dot(u, wgc_ref[...]) + bgc_ref[...]
    gc_ref[...] = gates(zc, lax.broadcasted_iota(I32, zc.shape, 1))
    zr = _dot_nt(wgr_ref[...], u) + bgr_ref[...]
    gr_ref[...] = gates(zr, lax.broadcasted_iota(I32, zr.shape, 0))

    dqt_ref[...] = (_dot_nt(wdq_ref[...], u) * (DSA_LATENT ** -0.5 * LOG2E)).astype(BF16)
    iqt_ref[...] = _dot_nt(wiq_ref[...], u).astype(BF16)
    wit_ref[...] = _dot_nt(wiw_ref[...], u) * (IDX_HEADS ** -0.5) * (IDX_DIM ** -0.5)
    ckv = _rmsnorm(_dot(u, wdc_ref[...]), kvg_ref[...])
    ckv_ref[...] = ckv.astype(BF16)
    ckvt_ref[0:DSA_LATENT, :] = ckv.T.astype(BF16)
    ckvt_ref[DSA_LATENT:, :] = (lax.broadcasted_iota(I32, (CKVT_ROWS - DSA_LATENT, tm), 0) == 0).astype(BF16)
    ik = _dot(u, wik_ref[...])[:, :IDX_DIM]
    ki_ref[...] = _rmsnorm(ik, ikg_ref[...]).astype(BF16)


def _proj(h1, g, w, seq):
    t, d = h1.shape
    tm = PROJ_TM
    row = lambda n: pl.BlockSpec((tm, n), lambda i: (i, 0))
    col = lambda n: pl.BlockSpec((n, tm), lambda i: (0, i))
    outs = [
        ("q", row(ML_WIDTH), (t, ML_WIDTH), BF16),
        ("k", row(ML_WIDTH), (t, ML_WIDTH), BF16),
        ("v", row(ML_WIDTH), (t, ML_WIDTH), BF16),
        ("o", row(ML_WIDTH), (t, ML_WIDTH), F32),
        ("gc", row(128), (t, 128), F32),
        ("gr", col(16), (16, t), F32),
        ("dqt", col(DSA_HEADS * DSA_LATENT), (DSA_HEADS * DSA_LATENT, t), BF16),
        ("iqt", col(IDX_HEADS * IDX_DIM), (IDX_HEADS * IDX_DIM, t), BF16),
        ("wit", col(16), (16, t), F32),
        ("ckv", row(DSA_LATENT), (t, DSA_LATENT), BF16),
        ("ckvt", col(CKVT_ROWS), (CKVT_ROWS, t), BF16),
        ("ki", row(IDX_DIM), (t, IDX_DIM), BF16),
    ]
    ins = [h1, g, w["wqk"], w["wv"], w["wo"], w["wgc"], w["wgr"], w["bgc"], w["bgr"], w["cw"], w["cb"],
           w["wdq_t"], w["wiq_t"], w["wiw_t"], w["wdc"], w["kvg"], w["wik"], w["ikg"]]
    res = pl.pallas_call(
        functools.partial(_proj_kernel, tiles_per_seq=seq // tm),
        grid=(t // tm,),
        in_specs=[row(d)] + [_full(a.shape) for a in ins[1:]],
        out_specs=[o[1] for o in outs],
        out_shape=[jax.ShapeDtypeStruct(o[2], o[3]) for o in outs],
        scratch_shapes=[pltpu.VMEM((tm + 8, 2 * ML_WIDTH), F32)],
        compiler_params=_params("arbitrary"),
        name="proj",
    )(*ins)
    return {o[0]: r for o, r in zip(outs, res)}


def _split3(x):
    hi = x.astype(BF16)
    r1 = x - hi.astype(F32)
    mid = r1.astype(BF16)
    lo = (r1 - mid.astype(F32)).astype(BF16)
    return hi, mid, lo


def _mlstm_kernel(q_ref, k_ref, v_ref, o_ref, gc_ref, gr_ref, hg_ref, y_ref, c_ref, m_ref):
    seq = q_ref.shape[0]
    L = ML_CHUNK
    d = ML_HEAD_DIM
    c_ref[...] = jnp.zeros(c_ref.shape, F32)
    m_ref[...] = jnp.zeros(m_ref.shape, F32)

    ri = lax.broadcasted_iota(I32, (L, L), 0)
    ci = lax.broadcasted_iota(I32, (L, L), 1)
    causal = ci <= ri
    tri_lo = causal.astype(BF16)
    tri_up = (ri <= ci).astype(BF16)
    ones_col = (lax.broadcasted_iota(I32, (L, d), 1) == 0).astype(F32)

    def chunk(c, carry):
        r0 = pl.multiple_of(c * L, L)
        gc = gc_ref[pl.ds(r0, L), :]
        gr = gr_ref[:, pl.ds(r0, L)]
        bc = sum(_dot(tri_lo, p) for p in _split3(gc))
        br = sum(_dot(p, tri_up) for p in _split3(gr))
        for h in range(ML_HEADS):
            hs = slice(h * d, (h + 1) * d)
            qh = q_ref[pl.ds(r0, L), hs]
            kh = k_ref[pl.ds(r0, L), hs]
            vh = v_ref[pl.ds(r0, L), hs].astype(F32)
            m_prev = m_ref[h:h + 1, 0:1]
            b_col = bc[:, ML_HEADS + h:ML_HEADS + h + 1]
            b_row = br[ML_HEADS + h:ML_HEADS + h + 1, :]
            i_col = gc[:, h:h + 1]
            i_row = gr[h:h + 1, :]
            logw = jnp.where(causal, b_col - b_row + i_row, NEG_INF)
            inter = b_col + m_prev
            mj = jnp.maximum(inter, jnp.max(logw, axis=1, keepdims=True))
            w = jnp.exp(logw - mj)
            a = jnp.exp(inter - mj)
            sqk = _dot_nt(qh, kh) * w
            ct = c_ref[h]
            v_aug = jnp.concatenate([vh, ones_col], axis=1)
            tot = a * _dot(qh, ct.astype(BF16)) + _dot(sqk.astype(BF16), v_aug.astype(BF16))
            num = tot[:, :d]
            den = tot[:, d:d + 1]
            hh = num / jnp.maximum(jnp.abs(den), jnp.exp(-mj))
            hn = _rmsnorm(hh, hg_ref[:, hs])
            y_ref[pl.ds(r0, L), hs] = (hn * _sigmoid(o_ref[pl.ds(r0, L), hs])).astype(BF16)
            b_last = b_row[:, L - 1:L]
            g_col = b_last - b_col + i_col
            g_row = b_last - b_row + i_row
            m_new = jnp.maximum(b_last + m_prev, jnp.max(g_row, axis=1, keepdims=True))
            wg = jnp.exp(g_col - m_new)
            dec = jnp.exp(b_last + m_prev - m_new)
            kt = kh.astype(F32).T.astype(BF16)
            c_ref[h] = dec * ct + _dot(kt, (wg * v_aug).astype(BF16))
            m_ref[h:h + 1, :] = jnp.broadcast_to(m_new, (1, m_ref.shape[1]))
        return carry

    lax.fori_loop(0, seq // L, chunk, 0)


def _mlstm(p, head_g, batch, seq):
    t = batch * seq
    row = lambda n: pl.BlockSpec((seq, n), lambda b: (b, 0))
    return pl.pallas_call(
        _mlstm_kernel,
        grid=(batch,),
        in_specs=[row(ML_WIDTH), row(ML_WIDTH), row(ML_WIDTH), row(ML_WIDTH), row(128),
                  pl.BlockSpec((16, seq), lambda b: (0, b)), _full((1, ML_WIDTH))],
        out_specs=row(ML_WIDTH),
        out_shape=jax.ShapeDtypeStruct((t, ML_WIDTH), BF16),
        scratch_shapes=[pltpu.VMEM((ML_HEADS, ML_HEAD_DIM, 2 * ML_HEAD_DIM), F32),
                        pltpu.VMEM((8, 128), F32)],
        compiler_params=_params("parallel"),
        name="mlstm",
    )(p["q"], p["k"], p["v"], p["o"], p["gc"], p["gr"], head_g)


def _float_of_code(code):
    key = code ^ jnp.int32(INT_MIN)
    bits = jnp.where(key < 0, key ^ jnp.int32(0x7FFFFFFF), key)
    return lax.bitcast_convert_type(bits, F32)


def _colmax8(x):
    parts = [x[8 * j:8 * j + 8, :] for j in range(x.shape[0] // 8)]
    while len(parts) > 1:
        parts = [jnp.maximum(parts[j], parts[j + 1]) for j in range(0, len(parts) - 1, 2)] + (
            [parts[-1]] if len(parts) % 2 else [])
    return parts[0]


def _dsa_kernel(iqt_ref, wit_ref, dqt_ref, ki_ref, ckv_ref, ckvt_ref, wuvt_ref, y_ref,
                iqa_ref, dqa_ref, score_ref, bias_ref, lg_ref, acc_ref, code_ref, yt_ref, *, topk, nq):
    tq = iqt_ref.shape[1]
    nh = DSA_HEADS
    nk = pl.program_id(1) + 1
    last = nk - 1
    rows = lambda c: pl.ds(pl.multiple_of(c * tq, tq), tq)
    hs = lambda h: slice(h * tq, (h + 1) * tq)
    dmat = lax.broadcasted_iota(I32, (tq, tq), 0) - lax.broadcasted_iota(I32, (tq, tq), 1)
    allowed = lambda c: dmat <= jnp.where(c < last, tq, 0)

    for h in range(nh):
        iqa_ref[:, hs(h)] = iqt_ref[h * IDX_DIM:(h + 1) * IDX_DIM, :]
        dqa_ref[:, hs(h)] = dqt_ref[h * DSA_LATENT:(h + 1) * DSA_LATENT, :]
    wi_all = jnp.concatenate([wit_ref[h:h + 1, :] for h in range(IDX_HEADS)], axis=1)

    def idx_block(c, carry):
        w = wi_all * jnp.maximum(_dot(ki_ref[rows(c), :], iqa_ref[...]), 0.0)
        acc = w[:, hs(0)]
        for h in range(1, IDX_HEADS):
            acc = acc + w[:, hs(h)]
        score_ref[rows(c), :] = jnp.where(allowed(c), acc, NEG_INF)
        return carry

    lax.fori_loop(0, nk, idx_block, 0)

    def search(n):
        def bit_pass(i, code):
            cand_code = code | lax.shift_left(jnp.int32(1), 31 - i)
            cand = _float_of_code(cand_code)
            cnt = jnp.sum(jnp.where(score_ref[0:n * tq, :] < cand, 0.0, 1.0), axis=0, keepdims=True)
            return jnp.where(cnt >= topk, cand_code, code)

        code_ref[0:1, :] = lax.fori_loop(0, 32, bit_pass, jnp.zeros((1, tq), I32))

    for n in range(1, nq + 1):
        pl.when(nk == n)(functools.partial(search, n))
    thr = _float_of_code(code_ref[0:1, :])

    def mask_block(c, carry):
        n_gt, n_eq = carry
        s = score_ref[rows(c), :]
        bias_ref[rows(c), :] = jnp.where((s >= thr) & allowed(c), 0.0, NEG_INF)
        n_gt = n_gt + jnp.sum(jnp.where(s > thr, 1.0, 0.0), axis=0, keepdims=True)
        n_eq = n_eq + jnp.sum(jnp.where(s == thr, 1.0, 0.0), axis=0, keepdims=True)
        return n_gt, n_eq

    zero = jnp.zeros((1, tq), F32)
    n_gt, n_eq = lax.fori_loop(0, nk, mask_block, (zero, zero))
    room = topk - n_gt

    @pl.when(jnp.max(n_eq - room) > 0.0)
    def _():
        lo = (dmat > 0).astype(BF16)

        def tie_block(c, before):
            s = score_ref[rows(c), :]
            ef = jnp.where(s == thr, 1.0, 0.0)
            rank = before + _dot(lo, ef.astype(BF16))
            keep = ((s > thr) | ((s == thr) & (rank < room))) & allowed(c)
            bias_ref[rows(c), :] = jnp.where(keep, 0.0, NEG_INF)
            return before + jnp.sum(ef, axis=0, keepdims=True)

        lax.fori_loop(0, nk, tie_block, zero)

    def logit_block(c, m8):
        s_all = _dot(ckv_ref[rows(c), :], dqa_ref[...])
        b = bias_ref[rows(c), :]
        new = []
        for h in range(nh):
            s = s_all[:, hs(h)] + b
            lg_ref[rows(c), hs(h)] = s
            new.append(jnp.maximum(m8[:, hs(h)], _colmax8(s)))
        return jnp.concatenate(new, axis=1)

    m8 = lax.fori_loop(0, nk, logit_block, jnp.full((8, nh * tq), NEG_INF, F32))
    m = jnp.max(m8, axis=0, keepdims=True)
    acc_ref[...] = jnp.zeros(acc_ref.shape, F32)

    def prob_block(c, carry):
        ckvt = ckvt_ref[:, rows(c)]
        for h in range(nh):
            p = jnp.exp2(lg_ref[rows(c), hs(h)] - m[:, hs(h)]).astype(BF16)
            acc_ref[h] += _dot(ckvt, p)
        return carry

    lax.fori_loop(0, nk, prob_block, 0)
    for h in range(nh):
        ot = acc_ref[h, 0:DSA_LATENT, :] / acc_ref[h, DSA_LATENT:DSA_LATENT + 1, :]
        yt_ref[h * DSA_HEAD_DIM:(h + 1) * DSA_HEAD_DIM, :] = _dot(wuvt_ref[h], ot.astype(BF16))
    y_ref[...] = yt_ref[...].T.astype(BF16)


def _dsa(p, wuv_t, batch, seq):
    t = batch * seq
    tq = DSA_TQ
    nq = seq // tq
    topk = min(INDEX_TOPK, seq // 4)
    qcol = lambda n: pl.BlockSpec((n, tq), lambda b, i: (0, b * nq + i))
    return pl.pallas_call(
        functools.partial(_dsa_kernel, topk=float(topk), nq=nq),
        grid=(batch, nq),
        in_specs=[qcol(IDX_HEADS * IDX_DIM), qcol(16), qcol(DSA_HEADS * DSA_LATENT),
                  pl.BlockSpec((seq, IDX_DIM), lambda b, i: (b, 0)),
                  pl.BlockSpec((seq, DSA_LATENT), lambda b, i: (b, 0)),
                  pl.BlockSpec((CKVT_ROWS, seq), lambda b, i: (0, b)),
                  _full(wuv_t.shape)],
        out_specs=pl.BlockSpec((tq, DSA_WIDTH), lambda b, i: (b * nq + i, 0)),
        out_shape=jax.ShapeDtypeStruct((t, DSA_WIDTH), BF16),
        scratch_shapes=[pltpu.VMEM((IDX_DIM, IDX_HEADS * tq), BF16), pltpu.VMEM((DSA_LATENT, DSA_HEADS * tq), BF16),
                        pltpu.VMEM((seq, tq), F32), pltpu.VMEM((seq, tq), F32),
                        pltpu.VMEM((seq, DSA_HEADS * tq), F32), pltpu.VMEM((DSA_HEADS, CKVT_ROWS, tq), F32),
                        pltpu.VMEM((8, tq), I32), pltpu.VMEM((DSA_WIDTH, tq), F32)],
        compiler_params=_params("parallel", "arbitrary"),
        name="dsa",
    )(p["iqt"], p["wit"], p["dqt"], p["ki"], p["ckv"], p["ckvt"], wuv_t)


def _memkv_kernel(m_ref, g_ref, w_ref, o_ref):
    o_ref[...] = _dot(_rmsnorm(m_ref[...], g_ref[...]).astype(BF16), w_ref[...]).astype(BF16)


def _memkv(mem2d, g, w_kv, mem_len):
    rows, d = mem2d.shape
    return pl.pallas_call(
        _memkv_kernel,
        grid=(rows // mem_len,),
        in_specs=[pl.BlockSpec((mem_len, d), lambda b: (b, 0)), _full((1, d)), _full(w_kv.shape)],
        out_specs=pl.BlockSpec((mem_len, w_kv.shape[1]), lambda b: (b, 0)),
        out_shape=jax.ShapeDtypeStruct((rows, w_kv.shape[1]), BF16),
        compiler_params=_params("parallel"),
        name="memkv",
    )(mem2d, g, w_kv)


def _mixout_kernel(h_ref, yml_ref, ydsa_ref, woa_ref, wob_ref, g_ref, wq_ref, kv_ref, wo_ref, o_ref):
    d = h_ref.shape[1]
    hd = d // XA_HEADS
    h2 = h_ref[...] + _dot(yml_ref[...], woa_ref[...]) + _dot(ydsa_ref[...], wob_ref[...])
    q = _dot(_rmsnorm(h2, g_ref[...]).astype(BF16), wq_ref[...]).astype(BF16)
    heads = []
    for h in range(XA_HEADS):
        kh = kv_ref[:, h * hd:(h + 1) * hd]
        vh = kv_ref[:, d + h * hd:d + (h + 1) * hd]
        lg = _dot_nt(q[:, h * hd:(h + 1) * hd], kh) * (hd ** -0.5)
        p = jnp.exp(lg - jnp.max(lg, axis=1, keepdims=True))
        denom = jnp.sum(p, axis=1, keepdims=True)
        heads.append((_dot(p.astype(BF16), vh) / denom).astype(BF16))
    o_ref[...] = h2 + _dot(jnp.concatenate(heads, axis=1), wo_ref[...])


def _mixout(h1, yml, ydsa, w, kv, batch, seq, mem_len):
    t, d = h1.shape
    tm = MIX_TM
    nt = seq // tm
    row = lambda n: pl.BlockSpec((tm, n), lambda b, j: (b * nt + j, 0))
    return pl.pallas_call(
        _mixout_kernel,
        grid=(batch, nt),
        in_specs=[row(d), row(ML_WIDTH), row(DSA_WIDTH), _full(w["woa"].shape), _full(w["wob"].shape),
                  _full((1, d)), _full(w["wq"].shape),
                  pl.BlockSpec((mem_len, 2 * d), lambda b, j: (b, 0)), _full(w["wxo"].shape)],
        out_specs=row(d),
        out_shape=jax.ShapeDtypeStruct((t, d), F32),
        compiler_params=_params("parallel", "parallel"),
        name="mixout",
    )(h1, yml, ydsa, w["woa"], w["wob"], w["xg"], w["wq"], kv, w["wxo"])


def _layer(h, mem2d, p, batch, seq, mem_len, final_g):
    h1 = _ffn(h, p["f1g"], p["f1wg"], p["f1wu"], p["f1wd"], p["f1g"], final_norm=False)
    pr = _proj(h1, p["mixg"], p, seq)
    yml = _mlstm(pr, p["headg"], batch, seq)
    ydsa = _dsa(pr, p["wuv_t"], batch, seq)
    kv = _memkv(mem2d, p["memg"], p["wkv"], mem_len)
    h3 = _mixout(h1, yml, ydsa, p, kv, batch, seq, mem_len)
    fg = p["f2g"] if final_g is None else final_g
    return _ffn(h3, p["f2g"], p["f2wg"], p["f2wu"], p["f2wd"], fg, final_norm=final_g is not None)


def kernel(x, mem, ffn1_norm_g, ffn1_w_gate, ffn1_w_up, ffn1_w_down, mix_norm_g, w_in, mlstm_conv_w, mlstm_conv_b, mlstm_i_bias, mlstm_f_bias, mlstm_head_norm_g, dsa_kv_norm_g, idx_k_norm_g, dsa_w_uv, w_out, xattn_norm_g, mem_norm_g, xattn_w_q, xattn_w_kv, xattn_w_o, ffn2_norm_g, ffn2_w_gate, ffn2_w_up, ffn2_w_down, final_norm_g):
    batch, seq, d = x.shape
    mem_len = mem.shape[1]
    depth = w_in.shape[0]
    h = x.reshape(batch * seq, d)
    mem2d = mem.reshape(batch * mem_len, d)
    row = lambda a: a.reshape(1, -1).astype(F32)
    b16 = lambda a: a.astype(BF16)

    splits = (ML_WIDTH, ML_WIDTH, ML_WIDTH, ML_HEADS, ML_HEADS, ML_WIDTH, DSA_HEADS * DSA_LATENT, DSA_LATENT,
              IDX_HEADS * IDX_DIM, IDX_DIM, IDX_HEADS)
    offs = [0]
    for s in splits:
        offs.append(offs[-1] + s)

    for l in range(depth):
        wi = w_in[l]
        cols = [wi[:, offs[i]:offs[i + 1]] for i in range(len(splits))]
        mq, mk, mv, mi, mf, mo, dq, dc, iq, ik, iw = cols
        w_gate = jnp.concatenate([mi, mf], axis=1)
        b_gate = jnp.concatenate([mlstm_i_bias[l], mlstm_f_bias[l]])
        p = {
            "f1g": row(ffn1_norm_g[l]), "f1wg": b16(ffn1_w_gate[l]), "f1wu": b16(ffn1_w_up[l]), "f1wd": b16(ffn1_w_down[l]),
            "f2g": row(ffn2_norm_g[l]), "f2wg": b16(ffn2_w_gate[l]), "f2wu": b16(ffn2_w_up[l]), "f2wd": b16(ffn2_w_down[l]),
            "mixg": row(mix_norm_g[l]),
            "wqk": b16(jnp.concatenate([mq, mk], axis=1)), "wv": b16(mv), "wo": b16(mo),
            "wgc": b16(jnp.pad(w_gate, ((0, 0), (0, 128 - 2 * ML_HEADS)))),
            "wgr": b16(jnp.pad(w_gate.T, ((0, 16 - 2 * ML_HEADS), (0, 0)))),
            "bgc": jnp.pad(b_gate, (0, 128 - 2 * ML_HEADS)).reshape(1, 128).astype(F32),
            "bgr": jnp.pad(b_gate, (0, 16 - 2 * ML_HEADS)).reshape(16, 1).astype(F32),
            "cw": mlstm_conv_w[l].astype(F32), "cb": row(mlstm_conv_b[l]),
            "wdq_t": b16(dq.T), "wiq_t": b16(iq.T), "wiw_t": b16(jnp.pad(iw.T, ((0, 16 - IDX_HEADS), (0, 0)))),
            "wdc": b16(dc), "kvg": row(dsa_kv_norm_g[l]),
            "wik": b16(jnp.pad(ik, ((0, 0), (0, 128 - IDX_DIM)))), "ikg": row(idx_k_norm_g[l]),
            "headg": row(mlstm_head_norm_g[l]),
            "wuv_t": b16(jnp.swapaxes(dsa_w_uv[l], 1, 2)),
            "woa": b16(w_out[l][:ML_WIDTH]), "wob": b16(w_out[l][ML_WIDTH:]),
            "xg": row(xattn_norm_g[l]), "memg": row(mem_norm_g[l]),
            "wq": b16(xattn_w_q[l]), "wkv": b16(xattn_w_kv[l]), "wxo": b16(xattn_w_o[l]),
        }
        fg = row(final_norm_g) if l == depth - 1 else None
        h = _layer(h, mem2d, p, batch, seq, mem_len, fg)
    return h.reshape(batch, seq, d)
```

```python
import functools

import jax
import jax.numpy as jnp
from jax import lax
from jax.experimental import pallas as pl
from jax.experimental.pallas import tpu as pltpu

F32 = jnp.float32
BF16 = jnp.bfloat16
I32 = jnp.int32
EPS = 1e-6

ML_HEADS = 4
ML_HEAD_DIM = 128
ML_WIDTH = ML_HEADS * ML_HEAD_DIM
ML_CONV = 4
DSA_HEADS = 8
DSA_LATENT = 128
DSA_HEAD_DIM = 64
DSA_WIDTH = DSA_HEADS * DSA_HEAD_DIM
IDX_HEADS = 8
IDX_DIM = 64
INDEX_TOPK = 256
XA_HEADS = 4

FFN_TM = 512
FFN_FC = 256
PROJ_TM = 512
ML_CHUNK = 128
ML_SEQS = 2
DSA_TQ = 256
MIX_TM = 512

CKVT_ROWS = DSA_LATENT + 16

VMEM_LIMIT = 56 * 1024 * 1024
INT_MIN = -(2 ** 31)
NEG_INF = float("-inf")
LOG2E = 1.4426950408889634


def _rmsnorm(x, g):
    return x * lax.rsqrt(jnp.mean(x * x, axis=-1, keepdims=True) + EPS) * g


def _sigmoid(x):
    return 1.0 / (1.0 + jnp.exp(-x))


def _dot(a, b):
    return jnp.dot(a, b, preferred_element_type=F32)


def _dot_nt(a, b):
    return lax.dot_general(a, b, (((1,), (1,)), ((), ())), preferred_element_type=F32)


def _full(shape):
    return pl.BlockSpec(shape, lambda *_: (0,) * len(shape))


def _params(*sem):
    return pltpu.CompilerParams(dimension_semantics=sem, vmem_limit_bytes=VMEM_LIMIT)


def _ffn_kernel(x_ref, g_ref, wg_ref, wu_ref, wd_ref, fg_ref, o_ref, *, final_norm):
    x = x_ref[...]
    xn = _rmsnorm(x, g_ref[...]).astype(BF16)
    d_ff = wg_ref.shape[1]
    acc = jnp.zeros(x.shape, F32)
    for c in range(d_ff // FFN_FC):
        sl = slice(c * FFN_FC, (c + 1) * FFN_FC)
        gate = _dot(xn, wg_ref[:, sl])
        up = _dot(xn, wu_ref[:, sl])
        act = (gate * _sigmoid(gate) * up).astype(BF16)
        acc = acc + _dot(act, wd_ref[sl, :])
    y = x + 0.5 * acc
    if final_norm:
        y = _rmsnorm(y, fg_ref[...])
    o_ref[...] = y


def _ffn(x, g, wg, wu, wd, fg, final_norm):
    t, d = x.shape
    d_ff = wg.shape[1]
    return pl.pallas_call(
        functools.partial(_ffn_kernel, final_norm=final_norm),
        grid=(t // FFN_TM,),
        in_specs=[
            pl.BlockSpec((FFN_TM, d), lambda i: (i, 0)),
            _full((1, d)),
            _full((d, d_ff)),
            _full((d, d_ff)),
            _full((d_ff, d)),
            _full((1, d)),
        ],
        out_specs=pl.BlockSpec((FFN_TM, d), lambda i: (i, 0)),
        out_shape=jax.ShapeDtypeStruct((t, d), F32),
        compiler_params=_params("parallel"),
        name="ffn_final" if final_norm else "ffn",
    )(x, g, wg, wu, wd, fg)


def _proj_kernel(h_ref, g_ref, wqk_ref, wv_ref, wo_ref, wgc_ref, wgr_ref, bgc_ref, bgr_ref,
                 cw_ref, cb_ref, wdq_ref, wiq_ref, wiw_ref, wdc_ref, kvg_ref, wik_ref, ikg_ref,
                 q_ref, k_ref, v_ref, o_ref, gc_ref, gr_ref, dqt_ref, iqt_ref, wit_ref,
                 ckv_ref, ckvt_ref, ki_ref, zbuf, *, tiles_per_seq):
    tm = h_ref.shape[0]
    u = _rmsnorm(h_ref[...], g_ref[...]).astype(BF16)

    @pl.when(pl.program_id(0) % tiles_per_seq == 0)
    def _():
        zbuf[0:8, :] = jnp.zeros((8, zbuf.shape[1]), F32)

    zbuf[8:8 + tm, :] = _dot(u, wqk_ref[...])
    y = jnp.zeros((tm, zbuf.shape[1]), F32) + cb_ref[...]
    for j in range(ML_CONV):
        y = y + zbuf[5 + j:5 + j + tm, :] * cw_ref[j:j + 1, :]
    zbuf[0:8, :] = zbuf[tm:tm + 8, :]
    y = y * _sigmoid(y)
    q_ref[...] = y[:, :ML_WIDTH].astype(BF16)
    k_ref[...] = (y[:, ML_WIDTH:] * (ML_HEAD_DIM ** -0.5)).astype(BF16)
    v_ref[...] = _dot(u, wv_ref[...]).astype(BF16)
    o_ref[...] = _dot(u, wo_ref[...]).astype(BF16)

    def gates(z, idx):
        return jnp.where(idx < ML_HEADS, z, jnp.minimum(z, 0.0) - jnp.log(1.0 + jnp.exp(-jnp.abs(z))))

    zc = _dot(u, wgc_ref[...]) + bgc_ref[...]
    gc_ref[...] = gates(zc, lax.broadcasted_iota(I32, zc.shape, 1))
    zr = _dot_nt(wgr_ref[...], u) + bgr_ref[...]
    gr_ref[...] = gates(zr, lax.broadcasted_iota(I32, zr.shape, 0))

    dqt_ref[...] = (_dot_nt(wdq_ref[...], u) * (DSA_LATENT ** -0.5 * LOG2E)).astype(BF16)
    iqt_ref[...] = _dot_nt(wiq_ref[...], u).astype(BF16)
    wit_ref[...] = _dot_nt(wiw_ref[...], u) * (IDX_HEADS ** -0.5) * (IDX_DIM ** -0.5)
    ckv = _rmsnorm(_dot(u, wdc_ref[...]), kvg_ref[...])
    ckv_ref[...] = ckv.astype(BF16)
    ckvt_ref[0:DSA_LATENT, :] = ckv.T.astype(BF16)
    ckvt_ref[DSA_LATENT:, :] = (lax.broadcasted_iota(I32, (CKVT_ROWS - DSA_LATENT, tm), 0) == 0).astype(BF16)
    ik = _dot(u, wik_ref[...])[:, :IDX_DIM]
    ki_ref[...] = _rmsnorm(ik, ikg_ref[...]).astype(BF16)


def _proj(h1, g, w, seq):
    t, d = h1.shape
    tm = PROJ_TM
    row = lambda n: pl.BlockSpec((tm, n), lambda i: (i, 0))
    col = lambda n: pl.BlockSpec((n, tm), lambda i: (0, i))
    outs = [
        ("q", row(ML_WIDTH), (t, ML_WIDTH), BF16),
        ("k", row(ML_WIDTH), (t, ML_WIDTH), BF16),
        ("v", row(ML_WIDTH), (t, ML_WIDTH), BF16),
        ("o", row(ML_WIDTH), (t, ML_WIDTH), BF16),
        ("gc", row(128), (t, 128), F32),
        ("gr", col(16), (16, t), F32),
        ("dqt", col(DSA_HEADS * DSA_LATENT), (DSA_HEADS * DSA_LATENT, t), BF16),
        ("iqt", col(IDX_HEADS * IDX_DIM), (IDX_HEADS * IDX_DIM, t), BF16),
        ("wit", col(16), (16, t), F32),
        ("ckv", row(DSA_LATENT), (t, DSA_LATENT), BF16),
        ("ckvt", col(CKVT_ROWS), (CKVT_ROWS, t), BF16),
        ("ki", row(IDX_DIM), (t, IDX_DIM), BF16),
    ]
    ins = [h1, g, w["wqk"], w["wv"], w["wo"], w["wgc"], w["wgr"], w["bgc"], w["bgr"], w["cw"], w["cb"],
           w["wdq_t"], w["wiq_t"], w["wiw_t"], w["wdc"], w["kvg"], w["wik"], w["ikg"]]
    res = pl.pallas_call(
        functools.partial(_proj_kernel, tiles_per_seq=seq // tm),
        grid=(t // tm,),
        in_specs=[row(d)] + [_full(a.shape) for a in ins[1:]],
        out_specs=[o[1] for o in outs],
        out_shape=[jax.ShapeDtypeStruct(o[2], o[3]) for o in outs],
        scratch_shapes=[pltpu.VMEM((tm + 8, 2 * ML_WIDTH), F32)],
        compiler_params=_params("arbitrary"),
        name="proj",
    )(*ins)
    return {o[0]: r for o, r in zip(outs, res)}


def _split3(x):
    hi = x.astype(BF16)
    r1 = x - hi.astype(F32)
    mid = r1.astype(BF16)
    lo = (r1 - mid.astype(F32)).astype(BF16)
    return hi, mid, lo


def _mlstm_kernel(q_ref, k_ref, v_ref, o_ref, gc_ref, gr_ref, hg_ref, y_ref, c_ref, m_ref, *, seq):
    L = ML_CHUNK
    d = ML_HEAD_DIM
    c_ref[...] = jnp.zeros(c_ref.shape, F32)
    m_ref[...] = jnp.zeros(m_ref.shape, F32)

    ri = lax.broadcasted_iota(I32, (L, L), 0)
    ci = lax.broadcasted_iota(I32, (L, L), 1)
    causal = ci <= ri
    tri_lo = causal.astype(BF16)
    tri_up = (ri <= ci).astype(BF16)
    ones_col = (lax.broadcasted_iota(I32, (L, d), 1) == 0).astype(F32)

    nch = ML_SEQS * ML_HEADS
    causal_all = jnp.concatenate([causal] * nch, axis=0)
    stack = lambda parts: jnp.concatenate(parts, axis=0)
    col_of = lambda x4: stack([jnp.broadcast_to(x4[j:j + 1, :], (L, 1)) for j in range(nch)])
    gain = stack([jnp.broadcast_to(hg_ref[:, (j % ML_HEADS) * d:(j % ML_HEADS + 1) * d], (L, d)) for j in range(nch)])

    def chunk(c, carry):
        r0s = [pl.multiple_of(sq * seq + c * L, L) for sq in range(ML_SEQS)]
        hsl = lambda j: slice((j % ML_HEADS) * d, (j % ML_HEADS + 1) * d)
        rws = lambda j: pl.ds(r0s[j // ML_HEADS], L)
        bcs, brs, gcs, grs = [], [], [], []
        for sq in range(ML_SEQS):
            gc = gc_ref[pl.ds(r0s[sq], L), :]
            gr = gr_ref[:, pl.ds(r0s[sq], L)]
            bcs.append(sum(_dot(tri_lo, p) for p in _split3(gc)))
            brs.append(sum(_dot(p, tri_up) for p in _split3(gr)))
            gcs.append(gc)
            grs.append(gr)
        b_col = stack([bcs[j // ML_HEADS][:, ML_HEADS + j % ML_HEADS:ML_HEADS + j % ML_HEADS + 1] for j in range(nch)])
        i_col = stack([gcs[j // ML_HEADS][:, j % ML_HEADS:j % ML_HEADS + 1] for j in range(nch)])
        b_row4 = stack([brs[sq][ML_HEADS:2 * ML_HEADS, :] for sq in range(ML_SEQS)])
        i_row4 = stack([grs[sq][0:ML_HEADS, :] for sq in range(ML_SEQS)])
        u_row4 = i_row4 - b_row4
        u_blk = stack([jnp.broadcast_to(u_row4[j:j + 1, :], (L, L)) for j in range(nch)])
        m_prev4 = m_ref[0:nch, 0:1]
        logw = jnp.where(causal_all, b_col + u_blk, NEG_INF)
        inter = b_col + col_of(m_prev4)
        mj = jnp.maximum(inter, jnp.max(logw, axis=1, keepdims=True))
        w = jnp.exp(logw - mj)
        a = jnp.exp(inter - mj)
        qs = [q_ref[rws(j), hsl(j)] for j in range(nch)]
        ks = [k_ref[rws(j), hsl(j)] for j in range(nch)]
        vaugs = [jnp.concatenate([v_ref[rws(j), hsl(j)].astype(F32), ones_col], axis=1) for j in range(nch)]
        sqk = (stack([_dot_nt(qs[j], ks[j]) for j in range(nch)]) * w).astype(BF16)
        cts = [c_ref[j] for j in range(nch)]
        tot = a * stack([_dot(qs[j], cts[j].astype(BF16)) for j in range(nch)]) + stack(
            [_dot(sqk[j * L:(j + 1) * L, :], vaugs[j].astype(BF16)) for j in range(nch)])
        hh = tot[:, :d] / jnp.maximum(jnp.abs(tot[:, d:d + 1]), jnp.exp(-mj))
        gate = _sigmoid(stack([o_ref[rws(j), hsl(j)] for j in range(nch)]).astype(F32))
        y = (_rmsnorm(hh, gain) * gate).astype(BF16)
        for j in range(nch):
            y_ref[rws(j), hsl(j)] = y[j * L:(j + 1) * L, :]
        b_last4 = b_row4[:, L - 1:L]
        g_row4 = b_last4 + u_row4
        m_new4 = jnp.maximum(b_last4 + m_prev4, jnp.max(g_row4, axis=1, keepdims=True))
        dec4 = jnp.exp(b_last4 + m_prev4 - m_new4)
        wg = jnp.exp(col_of(b_last4 - m_new4) - b_col + i_col)
        for j in range(nch):
            kt = ks[j].astype(F32).T.astype(BF16)
            wv = (wg[j * L:(j + 1) * L, :] * vaugs[j]).astype(BF16)
            c_ref[j] = dec4[j:j + 1, :] * cts[j] + _dot(kt, wv)
        m_ref[0:nch, :] = jnp.broadcast_to(m_new4, (nch, m_ref.shape[1]))
        return carry

    lax.fori_loop(0, seq // L, chunk, 0)


def _mlstm(p, head_g, batch, seq):
    t = batch * seq
    rows = ML_SEQS * seq
    row = lambda n: pl.BlockSpec((rows, n), lambda b: (b, 0))
    return pl.pallas_call(
        functools.partial(_mlstm_kernel, seq=seq),
        grid=(batch // ML_SEQS,),
        in_specs=[row(ML_WIDTH), row(ML_WIDTH), row(ML_WIDTH), row(ML_WIDTH), row(128),
                  pl.BlockSpec((16, rows), lambda b: (0, b)), _full((1, ML_WIDTH))],
        out_specs=row(ML_WIDTH),
        out_shape=jax.ShapeDtypeStruct((t, ML_WIDTH), BF16),
        scratch_shapes=[pltpu.VMEM((ML_SEQS * ML_HEADS, ML_HEAD_DIM, 2 * ML_HEAD_DIM), F32),
                        pltpu.VMEM((ML_SEQS * ML_HEADS, 128), F32)],
        compiler_params=_params("parallel"),
        name="mlstm",
    )(p["q"], p["k"], p["v"], p["o"], p["gc"], p["gr"], head_g)


def _float_of_code(code):
    key = code ^ jnp.int32(INT_MIN)
    bits = jnp.where(key < 0, key ^ jnp.int32(0x7FFFFFFF), key)
    return lax.bitcast_convert_type(bits, F32)


def _colmax8(x):
    parts = [x[8 * j:8 * j + 8, :] for j in range(x.shape[0] // 8)]
    while len(parts) > 1:
        parts = [jnp.maximum(parts[j], parts[j + 1]) for j in range(0, len(parts) - 1, 2)] + (
            [parts[-1]] if len(parts) % 2 else [])
    return parts[0]


def _dsa_kernel(iqt_ref, wit_ref, dqt_ref, ki_ref, ckv_ref, ckvt_ref, wuvt_ref, y_ref,
                iqa_ref, dqa_ref, score_ref, bias_ref, lg_ref, acc_ref, code_ref, yt_ref, *, topk, nq):
    tq = iqt_ref.shape[1]
    nh = DSA_HEADS
    nk = pl.program_id(1) + 1
    last = nk - 1
    rows = lambda c: pl.ds(pl.multiple_of(c * tq, tq), tq)
    hs = lambda h: slice(h * tq, (h + 1) * tq)
    dmat = lax.broadcasted_iota(I32, (tq, tq), 0) - lax.broadcasted_iota(I32, (tq, tq), 1)
    allowed = lambda c: dmat <= jnp.where(c < last, tq, 0)

    for h in range(nh):
        iqa_ref[:, hs(h)] = iqt_ref[h * IDX_DIM:(h + 1) * IDX_DIM, :]
        dqa_ref[:, hs(h)] = dqt_ref[h * DSA_LATENT:(h + 1) * DSA_LATENT, :]
    wi_all = jnp.concatenate([wit_ref[h:h + 1, :] for h in range(IDX_HEADS)], axis=1)

    def idx_block(c, carry):
        w = wi_all * jnp.maximum(_dot(ki_ref[rows(c), :], iqa_ref[...]), 0.0)
        acc = w[:, hs(0)]
        for h in range(1, IDX_HEADS):
            acc = acc + w[:, hs(h)]
        score_ref[rows(c), :] = jnp.where(allowed(c), acc, NEG_INF)
        return carry

    lax.fori_loop(0, nk, idx_block, 0)

    def search(n):
        def bit_pass(i, code):
            cand_code = code | lax.shift_left(jnp.int32(1), 31 - i)
            cand = _float_of_code(cand_code)
            cnt = jnp.sum(jnp.where(score_ref[0:n * tq, :] < cand, 0.0, 1.0), axis=0, keepdims=True)
            return jnp.where(cnt >= topk, cand_code, code)

        code_ref[0:1, :] = lax.fori_loop(0, 32, bit_pass, jnp.zeros((1, tq), I32))

    for n in range(1, nq + 1):
        pl.when(nk == n)(functools.partial(search, n))
    thr = _float_of_code(code_ref[0:1, :])

    def mask_block(c, carry):
        n_gt, n_eq = carry
        s = score_ref[rows(c), :]
        bias_ref[rows(c), :] = jnp.where((s >= thr) & allowed(c), 0.0, NEG_INF)
        n_gt = n_gt + jnp.sum(jnp.where(s > thr, 1.0, 0.0), axis=0, keepdims=True)
        n_eq = n_eq + jnp.sum(jnp.where(s == thr, 1.0, 0.0), axis=0, keepdims=True)
        return n_gt, n_eq

    zero = jnp.zeros((1, tq), F32)
    n_gt, n_eq = lax.fori_loop(0, nk, mask_block, (zero, zero))
    room = topk - n_gt

    @pl.when(jnp.max(n_eq - room) > 0.0)
    def _():
        lo = (dmat > 0).astype(BF16)

        def tie_block(c, before):
            s = score_ref[rows(c), :]
            ef = jnp.where(s == thr, 1.0, 0.0)
            rank = before + _dot(lo, ef.astype(BF16))
            keep = ((s > thr) | ((s == thr) & (rank < room))) & allowed(c)
            bias_ref[rows(c), :] = jnp.where(keep, 0.0, NEG_INF)
            return before + jnp.sum(ef, axis=0, keepdims=True)

        lax.fori_loop(0, nk, tie_block, zero)

    def logit_block(c, m8):
        s_all = _dot(ckv_ref[rows(c), :], dqa_ref[...])
        b = bias_ref[rows(c), :]
        new = []
        for h in range(nh):
            s = s_all[:, hs(h)] + b
            lg_ref[rows(c), hs(h)] = s
            new.append(jnp.maximum(m8[:, hs(h)], _colmax8(s)))
        return jnp.concatenate(new, axis=1)

    m8 = lax.fori_loop(0, nk, logit_block, jnp.full((8, nh * tq), NEG_INF, F32))
    m = jnp.max(m8, axis=0, keepdims=True)
    acc_ref[...] = jnp.zeros(acc_ref.shape, F32)

    def prob_block(c, carry):
        ckvt = ckvt_ref[:, rows(c)]
        for h in range(nh):
            p = jnp.exp2(lg_ref[rows(c), hs(h)] - m[:, hs(h)]).astype(BF16)
            acc_ref[h] += _dot(ckvt, p)
        return carry

    lax.fori_loop(0, nk, prob_block, 0)
    for h in range(nh):
        ot = acc_ref[h, 0:DSA_LATENT, :] / acc_ref[h, DSA_LATENT:DSA_LATENT + 1, :]
        yt_ref[h * DSA_HEAD_DIM:(h + 1) * DSA_HEAD_DIM, :] = _dot(wuvt_ref[h], ot.astype(BF16))
    y_ref[...] = yt_ref[...].T.astype(BF16)


def _dsa(p, wuv_t, batch, seq):
    t = batch * seq
    tq = DSA_TQ
    nq = seq // tq
    topk = min(INDEX_TOPK, seq // 4)
    qcol = lambda n: pl.BlockSpec((n, tq), lambda b, i: (0, b * nq + i))
    return pl.pallas_call(
        functools.partial(_dsa_kernel, topk=float(topk), nq=nq),
        grid=(batch, nq),
        in_specs=[qcol(IDX_HEADS * IDX_DIM), qcol(16), qcol(DSA_HEADS * DSA_LATENT),
                  pl.BlockSpec((seq, IDX_DIM), lambda b, i: (b, 0)),
                  pl.BlockSpec((seq, DSA_LATENT), lambda b, i: (b, 0)),
                  pl.BlockSpec((CKVT_ROWS, seq), lambda b, i: (0, b)),
                  _full(wuv_t.shape)],
        out_specs=pl.BlockSpec((tq, DSA_WIDTH), lambda b, i: (b * nq + i, 0)),
        out_shape=jax.ShapeDtypeStruct((t, DSA_WIDTH), BF16),
        scratch_shapes=[pltpu.VMEM((IDX_DIM, IDX_HEADS * tq), BF16), pltpu.VMEM((DSA_LATENT, DSA_HEADS * tq), BF16),
                        pltpu.VMEM((seq, tq), F32), pltpu.VMEM((seq, tq), F32),
                        pltpu.VMEM((seq, DSA_HEADS * tq), F32), pltpu.VMEM((DSA_HEADS, CKVT_ROWS, tq), F32),
                        pltpu.VMEM((8, tq), I32), pltpu.VMEM((DSA_WIDTH, tq), F32)],
        compiler_params=_params("parallel", "arbitrary"),
        name="dsa",
    )(p["iqt"], p["wit"], p["dqt"], p["ki"], p["ckv"], p["ckvt"], wuv_t)


def _memkv_kernel(m_ref, g_ref, w_ref, o_ref):
    o_ref[...] = _dot(_rmsnorm(m_ref[...], g_ref[...]).astype(BF16), w_ref[...]).astype(BF16)


def _memkv(mem2d, g, w_kv, mem_len):
    rows, d = mem2d.shape
    return pl.pallas_call(
        _memkv_kernel,
        grid=(rows // mem_len,),
        in_specs=[pl.BlockSpec((mem_len, d), lambda b: (b, 0)), _full((1, d)), _full(w_kv.shape)],
        out_specs=pl.BlockSpec((mem_len, w_kv.shape[1]), lambda b: (b, 0)),
        out_shape=jax.ShapeDtypeStruct((rows, w_kv.shape[1]), BF16),
        compiler_params=_params("parallel"),
        name="memkv",
    )(mem2d, g, w_kv)


def _mixout_kernel(h_ref, yml_ref, ydsa_ref, woa_ref, wob_ref, g_ref, wq_ref, kv_ref, wo_ref, o_ref):
    d = h_ref.shape[1]
    hd = d // XA_HEADS
    h2 = h_ref[...] + _dot(yml_ref[...], woa_ref[...]) + _dot(ydsa_ref[...], wob_ref[...])
    q = _dot(_rmsnorm(h2, g_ref[...]).astype(BF16), wq_ref[...]).astype(BF16)
    heads = []
    for h in range(XA_HEADS):
        kh = kv_ref[:, h * hd:(h + 1) * hd]
        vh = kv_ref[:, d + h * hd:d + (h + 1) * hd]
        lg = _dot_nt(q[:, h * hd:(h + 1) * hd], kh) * (hd ** -0.5)
        p = jnp.exp(lg - jnp.max(lg, axis=1, keepdims=True))
        denom = jnp.sum(p, axis=1, keepdims=True)
        heads.append((_dot(p.astype(BF16), vh) / denom).astype(BF16))
    o_ref[...] = h2 + _dot(jnp.concatenate(heads, axis=1), wo_ref[...])


def _mixout(h1, yml, ydsa, w, kv, batch, seq, mem_len):
    t, d = h1.shape
    tm = MIX_TM
    nt = seq // tm
    row = lambda n: pl.BlockSpec((tm, n), lambda b, j: (b * nt + j, 0))
    return pl.pallas_call(
        _mixout_kernel,
        grid=(batch, nt),
        in_specs=[row(d), row(ML_WIDTH), row(DSA_WIDTH), _full(w["woa"].shape), _full(w["wob"].shape),
                  _full((1, d)), _full(w["wq"].shape),
                  pl.BlockSpec((mem_len, 2 * d), lambda b, j: (b, 0)), _full(w["wxo"].shape)],
        out_specs=row(d),
        out_shape=jax.ShapeDtypeStruct((t, d), F32),
        compiler_params=_params("parallel", "parallel"),
        name="mixout",
    )(h1, yml, ydsa, w["woa"], w["wob"], w["xg"], w["wq"], kv, w["wxo"])


def _layer(h, mem2d, p, batch, seq, mem_len, final_g):
    h1 = _ffn(h, p["f1g"], p["f1wg"], p["f1wu"], p["f1wd"], p["f1g"], final_norm=False)
    pr = _proj(h1, p["mixg"], p, seq)
    yml = _mlstm(pr, p["headg"], batch, seq)
    ydsa = _dsa(pr, p["wuv_t"], batch, seq)
    kv = _memkv(mem2d, p["memg"], p["wkv"], mem_len)
    h3 = _mixout(h1, yml, ydsa, p, kv, batch, seq, mem_len)
    fg = p["f2g"] if final_g is None else final_g
    return _ffn(h3, p["f2g"], p["f2wg"], p["f2wu"], p["f2wd"], fg, final_norm=final_g is not None)


def kernel(x, mem, ffn1_norm_g, ffn1_w_gate, ffn1_w_up, ffn1_w_down, mix_norm_g, w_in, mlstm_conv_w, mlstm_conv_b, mlstm_i_bias, mlstm_f_bias, mlstm_head_norm_g, dsa_kv_norm_g, idx_k_norm_g, dsa_w_uv, w_out, xattn_norm_g, mem_norm_g, xattn_w_q, xattn_w_kv, xattn_w_o, ffn2_norm_g, ffn2_w_gate, ffn2_w_up, ffn2_w_down, final_norm_g):
    batch, seq, d = x.shape
    mem_len = mem.shape[1]
    depth = w_in.shape[0]
    h = x.reshape(batch * seq, d)
    mem2d = mem.reshape(batch * mem_len, d)
    row = lambda a: a.reshape(1, -1).astype(F32)
    b16 = lambda a: a.astype(BF16)

    splits = (ML_WIDTH, ML_WIDTH, ML_WIDTH, ML_HEADS, ML_HEADS, ML_WIDTH, DSA_HEADS * DSA_LATENT, DSA_LATENT,
              IDX_HEADS * IDX_DIM, IDX_DIM, IDX_HEADS)
    offs = [0]
    for s in splits:
        offs.append(offs[-1] + s)

    for l in range(depth):
        wi = w_in[l]
        cols = [wi[:, offs[i]:offs[i + 1]] for i in range(len(splits))]
        mq, mk, mv, mi, mf, mo, dq, dc, iq, ik, iw = cols
        w_gate = jnp.concatenate([mi, mf], axis=1)
        b_gate = jnp.concatenate([mlstm_i_bias[l], mlstm_f_bias[l]])
        p = {
            "f1g": row(ffn1_norm_g[l]), "f1wg": b16(ffn1_w_gate[l]), "f1wu": b16(ffn1_w_up[l]), "f1wd": b16(ffn1_w_down[l]),
            "f2g": row(ffn2_norm_g[l]), "f2wg": b16(ffn2_w_gate[l]), "f2wu": b16(ffn2_w_up[l]), "f2wd": b16(ffn2_w_down[l]),
            "mixg": row(mix_norm_g[l]),
            "wqk": b16(jnp.concatenate([mq, mk], axis=1)), "wv": b16(mv), "wo": b16(mo),
            "wgc": b16(jnp.pad(w_gate, ((0, 0), (0, 128 - 2 * ML_HEADS)))),
            "wgr": b16(jnp.pad(w_gate.T, ((0, 16 - 2 * ML_HEADS), (0, 0)))),
            "bgc": jnp.pad(b_gate, (0, 128 - 2 * ML_HEADS)).reshape(1, 128).astype(F32),
            "bgr": jnp.pad(b_gate, (0, 16 - 2 * ML_HEADS)).reshape(16, 1).astype(F32),
            "cw": mlstm_conv_w[l].astype(F32), "cb": row(mlstm_conv_b[l]),
            "wdq_t": b16(dq.T), "wiq_t": b16(iq.T), "wiw_t": b16(jnp.pad(iw.T, ((0, 16 - IDX_HEADS), (0, 0)))),
            "wdc": b16(dc), "kvg": row(dsa_kv_norm_g[l]),
            "wik": b16(jnp.pad(ik, ((0, 0), (0, 128 - IDX_DIM)))), "ikg": row(idx_k_norm_g[l]),
            "headg": row(mlstm_head_norm_g[l]),
            "wuv_t": b16(jnp.swapaxes(dsa_w_uv[l], 1, 2)),
            "woa": b16(w_out[l][:ML_WIDTH]), "wob": b16(w_out[l][ML_WIDTH:]),
            "xg": row(xattn_norm_g[l]), "memg": row(mem_norm_g[l]),
            "wq": b16(xattn_w_q[l]), "wkv": b16(xattn_w_kv[l]), "wxo": b16(xattn_w_o[l]),
        }
        fg = row(final_norm_g) if l == depth - 1 else None
        h = _layer(h, mem2d, p, batch, seq, mem_len, fg)
    return h.reshape(batch, seq, d)
```

```python
import functools

import jax
import jax.numpy as jnp
from jax import lax
from jax.experimental import pallas as pl
from jax.experimental.pallas import tpu as pltpu

F32 = jnp.float32
BF16 = jnp.bfloat16
I32 = jnp.int32
EPS = 1e-6

ML_HEADS = 4
ML_HEAD_DIM = 128
ML_WIDTH = ML_HEADS * ML_HEAD_DIM
ML_CONV = 4
DSA_HEADS = 8
DSA_LATENT = 128
DSA_HEAD_DIM = 64
DSA_WIDTH = DSA_HEADS * DSA_HEAD_DIM
IDX_HEADS = 8
IDX_DIM = 64
INDEX_TOPK = 256
XA_HEADS = 4

FFN_TM = 512
FFN_FC = 256
PROJ_TM = 512
ML_CHUNK = 128
ML_SEQS = 2
DSA_TQ = 256
MIX_TM = 512

CKVT_ROWS = DSA_LATENT + 16

VMEM_LIMIT = 56 * 1024 * 1024
INT_MIN = -(2 ** 31)
NEG_INF = float("-inf")
LOG2E = 1.4426950408889634


def _rmsnorm(x, g):
    return x * lax.rsqrt(jnp.mean(x * x, axis=-1, keepdims=True) + EPS) * g


def _sigmoid(x):
    return 1.0 / (1.0 + jnp.exp(-x))


def _dot(a, b):
    return jnp.dot(a, b, preferred_element_type=F32)


def _dot_nt(a, b):
    return lax.dot_general(a, b, (((1,), (1,)), ((), ())), preferred_element_type=F32)


def _full(shape):
    return pl.BlockSpec(shape, lambda *_: (0,) * len(shape))


def _params(*sem):
    return pltpu.CompilerParams(dimension_semantics=sem, vmem_limit_bytes=VMEM_LIMIT)


def _ffn_kernel(x_ref, g_ref, wg_ref, wu_ref, wd_ref, fg_ref, o_ref, *, final_norm):
    x = x_ref[...]
    xn = _rmsnorm(x, g_ref[...]).astype(BF16)
    d_ff = wg_ref.shape[1]
    acc = jnp.zeros(x.shape, F32)
    for c in range(d_ff // FFN_FC):
        sl = slice(c * FFN_FC, (c + 1) * FFN_FC)
        gate = _dot(xn, wg_ref[:, sl])
        up = _dot(xn, wu_ref[:, sl])
        act = (gate * _sigmoid(gate) * up).astype(BF16)
        acc = acc + _dot(act, wd_ref[sl, :])
    y = x + 0.5 * acc
    if final_norm:
        y = _rmsnorm(y, fg_ref[...])
    o_ref[...] = y


def _ffn(x, g, wg, wu, wd, fg, final_norm):
    t, d = x.shape
    d_ff = wg.shape[1]
    return pl.pallas_call(
        functools.partial(_ffn_kernel, final_norm=final_norm),
        grid=(t // FFN_TM,),
        in_specs=[
            pl.BlockSpec((FFN_TM, d), lambda i: (i, 0)),
            _full((1, d)),
            _full((d, d_ff)),
            _full((d, d_ff)),
            _full((d_ff, d)),
            _full((1, d)),
        ],
        out_specs=pl.BlockSpec((FFN_TM, d), lambda i: (i, 0)),
        out_shape=jax.ShapeDtypeStruct((t, d), F32),
        compiler_params=_params("parallel"),
        name="ffn_final" if final_norm else "ffn",
    )(x, g, wg, wu, wd, fg)


def _proj_kernel(h_ref, g_ref, wqk_ref, wv_ref, wo_ref, wgc_ref, wgr_ref, bgc_ref, bgr_ref,
                 cw_ref, cb_ref, wdq_ref, wiq_ref, wiw_ref, wdc_ref, kvg_ref, wik_ref, ikg_ref,
                 q_ref, k_ref, v_ref, o_ref, gc_ref, gr_ref, dqt_ref, iqt_ref, wit_ref,
                 ckv_ref, ckvt_ref, ki_ref, zbuf, *, tiles_per_seq):
    tm = h_ref.shape[0]
    u = _rmsnorm(h_ref[...], g_ref[...]).astype(BF16)

    @pl.when(pl.program_id(0) % tiles_per_seq == 0)
    def _():
        zbuf[0:8, :] = jnp.zeros((8, zbuf.shape[1]), F32)

    zbuf[8:8 + tm, :] = _dot(u, wqk_ref[...])
    y = jnp.zeros((tm, zbuf.shape[1]), F32) + cb_ref[...]
    for j in range(ML_CONV):
        y = y + zbuf[5 + j:5 + j + tm, :] * cw_ref[j:j + 1, :]
    zbuf[0:8, :] = zbuf[tm:tm + 8, :]
    y = y * _sigmoid(y)
    q_ref[...] = y[:, :ML_WIDTH].astype(BF16)
    k_ref[...] = (y[:, ML_WIDTH:] * (ML_HEAD_DIM ** -0.5)).astype(BF16)
    v_ref[...] = _dot(u, wv_ref[...]).astype(BF16)
    o_ref[...] = _dot(u, wo_ref[...]).astype(BF16)

    def gates(z, idx):
        return jnp.where(idx < ML_HEADS, z, jnp.minimum(z, 0.0) - jnp.log(1.0 + jnp.exp(-jnp.abs(z))))

    zc = _dot(u, wgc_ref[...]) + bgc_ref[...]
    gc_ref[...] = gates(zc, lax.broadcasted_iota(I32, zc.shape, 1))
    zr = _dot_nt(wgr_ref[...], u) + bgr_ref[...]
    gr_ref[...] = gates(zr, lax.broadcasted_iota(I32, zr.shape, 0))

    dqt_ref[...] = (_dot_nt(wdq_ref[...], u) * (DSA_LATENT ** -0.5 * LOG2E)).astype(BF16)
    iqt_ref[...] = _dot_nt(wiq_ref[...], u).astype(BF16)
    wit_ref[...] = _dot_nt(wiw_ref[...], u) * (IDX_HEADS ** -0.5) * (IDX_DIM ** -0.5)
    ckv = _rmsnorm(_dot(u, wdc_ref[...]), kvg_ref[...])
    ckv_ref[...] = ckv.astype(BF16)
    ckvt_ref[0:DSA_LATENT, :] = ckv.T.astype(BF16)
    ckvt_ref[DSA_LATENT:, :] = (lax.broadcasted_iota(I32, (CKVT_ROWS - DSA_LATENT, tm), 0) == 0).astype(BF16)
    ik = _dot(u, wik_ref[...])[:, :IDX_DIM]
    ki_ref[...] = _rmsnorm(ik, ikg_ref[...]).astype(BF16)


def _proj(h1, g, w, seq):
    t, d = h1.shape
    tm = PROJ_TM
    row = lambda n: pl.BlockSpec((tm, n), lambda i: (i, 0))
    col = lambda n: pl.BlockSpec((n, tm), lambda i: (0, i))
    outs = [
        ("q", row(ML_WIDTH), (t, ML_WIDTH), BF16),
        ("k", row(ML_WIDTH), (t, ML_WIDTH), BF16),
        ("v", row(ML_WIDTH), (t, ML_WIDTH), BF16),
        ("o", row(ML_WIDTH), (t, ML_WIDTH), BF16),
        ("gc", row(128), (t, 128), F32),
        ("gr", col(16), (16, t), F32),
        ("dqt", col(DSA_HEADS * DSA_LATENT), (DSA_HEADS * DSA_LATENT, t), BF16),
        ("iqt", col(IDX_HEADS * IDX_DIM), (IDX_HEADS * IDX_DIM, t), BF16),
        ("wit", col(16), (16, t), F32),
        ("ckv", row(DSA_LATENT), (t, DSA_LATENT), BF16),
        ("ckvt", col(CKVT_ROWS), (CKVT_ROWS, t), BF16),
        ("ki", row(IDX_DIM), (t, IDX_DIM), BF16),
    ]
    ins = [h1, g, w["wqk"], w["wv"], w["wo"], w["wgc"], w["wgr"], w["bgc"], w["bgr"], w["cw"], w["cb"],
           w["wdq_t"], w["wiq_t"], w["wiw_t"], w["wdc"], w["kvg"], w["wik"], w["ikg"]]
    res = pl.pallas_call(
        functools.partial(_proj_kernel, tiles_per_seq=seq // tm),
        grid=(t // tm,),
        in_specs=[row(d)] + [_full(a.shape) for a in ins[1:]],
        out_specs=[o[1] for o in outs],
        out_shape=[jax.ShapeDtypeStruct(o[2], o[3]) for o in outs],
        scratch_shapes=[pltpu.VMEM((tm + 8, 2 * ML_WIDTH), F32)],
        compiler_params=_params("arbitrary"),
        name="proj",
    )(*ins)
    return {o[0]: r for o, r in zip(outs, res)}


def _split3(x):
    hi = x.astype(BF16)
    r1 = x - hi.astype(F32)
    mid = r1.astype(BF16)
    lo = (r1 - mid.astype(F32)).astype(BF16)
    return hi, mid, lo


def _mlstm_kernel(q_ref, k_ref, v_ref, o_ref, gc_ref, gr_ref, hg_ref, y_ref, c_ref, m_ref, *, seq):
    L = ML_CHUNK
    d = ML_HEAD_DIM
    c_ref[...] = jnp.zeros(c_ref.shape, F32)
    m_ref[...] = jnp.zeros(m_ref.shape, F32)

    ri = lax.broadcasted_iota(I32, (L, L), 0)
    ci = lax.broadcasted_iota(I32, (L, L), 1)
    causal = ci <= ri
    tri_lo = causal.astype(BF16)
    tri_up = (ri <= ci).astype(BF16)
    ones_col = (lax.broadcasted_iota(I32, (L, d), 1) == 0).astype(F32)

    nch = ML_SEQS * ML_HEADS
    causal_all = jnp.concatenate([causal] * nch, axis=0)
    stack = lambda parts: jnp.concatenate(parts, axis=0)
    col_of = lambda x4: stack([jnp.broadcast_to(x4[j:j + 1, :], (L, 1)) for j in range(nch)])
    gain = stack([jnp.broadcast_to(hg_ref[:, (j % ML_HEADS) * d:(j % ML_HEADS + 1) * d], (L, d)) for j in range(nch)])

    def chunk(c, carry):
        r0s = [pl.multiple_of(sq * seq + c * L, L) for sq in range(ML_SEQS)]
        hsl = lambda j: slice((j % ML_HEADS) * d, (j % ML_HEADS + 1) * d)
        rws = lambda j: pl.ds(r0s[j // ML_HEADS], L)
        bcs, brs, gcs, grs = [], [], [], []
        for sq in range(ML_SEQS):
            gc = gc_ref[pl.ds(r0s[sq], L), :]
            gr = gr_ref[:, pl.ds(r0s[sq], L)]
            bcs.append(sum(_dot(tri_lo, p) for p in _split3(gc)))
            brs.append(sum(_dot(p, tri_up) for p in _split3(gr)))
            gcs.append(gc)
            grs.append(gr)
        b_col = stack([bcs[j // ML_HEADS][:, ML_HEADS + j % ML_HEADS:ML_HEADS + j % ML_HEADS + 1] for j in range(nch)])
        i_col = stack([gcs[j // ML_HEADS][:, j % ML_HEADS:j % ML_HEADS + 1] for j in range(nch)])
        b_row4 = stack([brs[sq][ML_HEADS:2 * ML_HEADS, :] for sq in range(ML_SEQS)])
        i_row4 = stack([grs[sq][0:ML_HEADS, :] for sq in range(ML_SEQS)])
        u_row4 = i_row4 - b_row4
        u_blk = stack([jnp.broadcast_to(u_row4[j:j + 1, :], (L, L)) for j in range(nch)])
        m_prev4 = m_ref[0:nch, 0:1]
        logw = jnp.where(causal_all, b_col + u_blk, NEG_INF)
        inter = b_col + col_of(m_prev4)
        mj = jnp.maximum(inter, jnp.max(logw, axis=1, keepdims=True))
        w = jnp.exp(logw - mj)
        a = jnp.exp(inter - mj)
        qs = [q_ref[rws(j), hsl(j)] for j in range(nch)]
        ks = [k_ref[rws(j), hsl(j)] for j in range(nch)]
        vaugs = [jnp.concatenate([v_ref[rws(j), hsl(j)].astype(F32), ones_col], axis=1) for j in range(nch)]
        sqk = (stack([_dot_nt(qs[j], ks[j]) for j in range(nch)]) * w).astype(BF16)
        cts = [c_ref[j] for j in range(nch)]
        tot = a * stack([_dot(qs[j], cts[j].astype(BF16)) for j in range(nch)]) + stack(
            [_dot(sqk[j * L:(j + 1) * L, :], vaugs[j].astype(BF16)) for j in range(nch)])
        hh = tot[:, :d] / jnp.maximum(jnp.abs(tot[:, d:d + 1]), jnp.exp(-mj))
        gate = _sigmoid(stack([o_ref[rws(j), hsl(j)] for j in range(nch)]).astype(F32))
        y = (_rmsnorm(hh, gain) * gate).astype(BF16)
        for j in range(nch):
            y_ref[rws(j), hsl(j)] = y[j * L:(j + 1) * L, :]
        b_last4 = b_row4[:, L - 1:L]
        g_row4 = b_last4 + u_row4
        m_new4 = jnp.maximum(b_last4 + m_prev4, jnp.max(g_row4, axis=1, keepdims=True))
        dec4 = jnp.exp(b_last4 + m_prev4 - m_new4)
        wg = jnp.exp(col_of(b_last4 - m_new4) - b_col + i_col)
        for j in range(nch):
            kt = ks[j].astype(F32).T.astype(BF16)
            wv = (wg[j * L:(j + 1) * L, :] * vaugs[j]).astype(BF16)
            c_ref[j] = dec4[j:j + 1, :] * cts[j] + _dot(kt, wv)
        m_ref[0:nch, :] = jnp.broadcast_to(m_new4, (nch, m_ref.shape[1]))
        return carry

    lax.fori_loop(0, seq // L, chunk, 0)


def _mlstm(p, head_g, batch, seq):
    t = batch * seq
    rows = ML_SEQS * seq
    row = lambda n: pl.BlockSpec((rows, n), lambda b: (b, 0))
    return pl.pallas_call(
        functools.partial(_mlstm_kernel, seq=seq),
        grid=(batch // ML_SEQS,),
        in_specs=[row(ML_WIDTH), row(ML_WIDTH), row(ML_WIDTH), row(ML_WIDTH), row(128),
                  pl.BlockSpec((16, rows), lambda b: (0, b)), _full((1, ML_WIDTH))],
        out_specs=row(ML_WIDTH),
        out_shape=jax.ShapeDtypeStruct((t, ML_WIDTH), BF16),
        scratch_shapes=[pltpu.VMEM((ML_SEQS * ML_HEADS, ML_HEAD_DIM, 2 * ML_HEAD_DIM), F32),
                        pltpu.VMEM((ML_SEQS * ML_HEADS, 128), F32)],
        compiler_params=_params("parallel"),
        name="mlstm",
    )(p["q"], p["k"], p["v"], p["o"], p["gc"], p["gr"], head_g)


def _float_of_code(code):
    key = code ^ jnp.int32(INT_MIN)
    bits = jnp.where(key < 0, key ^ jnp.int32(0x7FFFFFFF), key)
    return lax.bitcast_convert_type(bits, F32)


def _colmax8(x):
    parts = [x[8 * j:8 * j + 8, :] for j in range(x.shape[0] // 8)]
    while len(parts) > 1:
        parts = [jnp.maximum(parts[j], parts[j + 1]) for j in range(0, len(parts) - 1, 2)] + (
            [parts[-1]] if len(parts) % 2 else [])
    return parts[0]


def _colsum(x):
    groups = x.shape[0] // 8
    lanes = min(2, groups)
    parts = [x[8 * j:8 * j + 8, :] for j in range(lanes)]
    for j in range(lanes, groups):
        parts[j % lanes] = parts[j % lanes] + x[8 * j:8 * j + 8, :]
    while len(parts) > 1:
        parts = [parts[j] + parts[j + 1] for j in range(0, len(parts), 2)]
    return jnp.sum(parts[0], axis=0, keepdims=True)


def _dsa_kernel(iqt_ref, wit_ref, dqt_ref, ki_ref, ckv_ref, ckvt_ref, wuvt_ref, y_ref,
                iqa_ref, dqa_ref, score_ref, bias_ref, lg_ref, acc_ref, code_ref, yt_ref, *, topk, nq):
    tq = iqt_ref.shape[1]
    nh = DSA_HEADS
    nk = pl.program_id(1) + 1
    last = nk - 1
    rows = lambda c: pl.ds(pl.multiple_of(c * tq, tq), tq)
    hs = lambda h: slice(h * tq, (h + 1) * tq)
    dmat = lax.broadcasted_iota(I32, (tq, tq), 0) - lax.broadcasted_iota(I32, (tq, tq), 1)
    allowed = lambda c: dmat <= jnp.where(c < last, tq, 0)

    for h in range(nh):
        iqa_ref[:, hs(h)] = iqt_ref[h * IDX_DIM:(h + 1) * IDX_DIM, :]
        dqa_ref[:, hs(h)] = dqt_ref[h * DSA_LATENT:(h + 1) * DSA_LATENT, :]
    wi_all = jnp.concatenate([wit_ref[h:h + 1, :] for h in range(IDX_HEADS)], axis=1)

    def idx_block(c, carry):
        w = wi_all * jnp.maximum(_dot(ki_ref[rows(c), :], iqa_ref[...]), 0.0)
        acc = w[:, hs(0)]
        for h in range(1, IDX_HEADS):
            acc = acc + w[:, hs(h)]
        score_ref[rows(c), :] = jnp.where(allowed(c), acc, NEG_INF)
        return carry

    lax.fori_loop(0, nk, idx_block, 0)

    def search(n):
        def bit_pass(i, code):
            cand_code = code | lax.shift_left(jnp.int32(1), 31 - i)
            cand = _float_of_code(cand_code)
            cnt = _colsum(jnp.where(score_ref[0:n * tq, :] < cand, 0.0, 1.0))
            return jnp.where(cnt >= topk, cand_code, code)

        code_ref[0:1, :] = lax.fori_loop(0, 32, bit_pass, jnp.zeros((1, tq), I32))

    for n in range(1, nq + 1):
        pl.when(nk == n)(functools.partial(search, n))
    thr = _float_of_code(code_ref[0:1, :])

    def mask_block(c, carry):
        n_gt, n_eq = carry
        s = score_ref[rows(c), :]
        bias_ref[rows(c), :] = jnp.where((s >= thr) & allowed(c), 0.0, NEG_INF)
        n_gt = n_gt + _colsum(jnp.where(s > thr, 1.0, 0.0))
        n_eq = n_eq + _colsum(jnp.where(s == thr, 1.0, 0.0))
        return n_gt, n_eq

    zero = jnp.zeros((1, tq), F32)
    n_gt, n_eq = lax.fori_loop(0, nk, mask_block, (zero, zero))
    room = topk - n_gt

    @pl.when(jnp.max(n_eq - room) > 0.0)
    def _():
        lo = (dmat > 0).astype(BF16)

        def tie_block(c, before):
            s = score_ref[rows(c), :]
            ef = jnp.where(s == thr, 1.0, 0.0)
            rank = before + _dot(lo, ef.astype(BF16))
            keep = ((s > thr) | ((s == thr) & (rank < room))) & allowed(c)
            bias_ref[rows(c), :] = jnp.where(keep, 0.0, NEG_INF)
            return before + jnp.sum(ef, axis=0, keepdims=True)

        lax.fori_loop(0, nk, tie_block, zero)

    def logit_block(c, m8):
        s_all = _dot(ckv_ref[rows(c), :], dqa_ref[...])
        b = bias_ref[rows(c), :]
        new = []
        for h in range(nh):
            s = s_all[:, hs(h)] + b
            lg_ref[rows(c), hs(h)] = s
            new.append(jnp.maximum(m8[:, hs(h)], _colmax8(s)))
        return jnp.concatenate(new, axis=1)

    m8 = lax.fori_loop(0, nk, logit_block, jnp.full((8, nh * tq), NEG_INF, F32))
    m = jnp.max(m8, axis=0, keepdims=True)
    acc_ref[...] = jnp.zeros(acc_ref.shape, F32)

    def prob_block(c, carry):
        ckvt = ckvt_ref[:, rows(c)]
        for h in range(nh):
            p = jnp.exp2(lg_ref[rows(c), hs(h)] - m[:, hs(h)]).astype(BF16)
            acc_ref[h] += _dot(ckvt, p)
        return carry

    lax.fori_loop(0, nk, prob_block, 0)
    for h in range(nh):
        ot = acc_ref[h, 0:DSA_LATENT, :] / acc_ref[h, DSA_LATENT:DSA_LATENT + 1, :]
        yt_ref[h * DSA_HEAD_DIM:(h + 1) * DSA_HEAD_DIM, :] = _dot(wuvt_ref[h], ot.astype(BF16))
    y_ref[...] = yt_ref[...].T.astype(BF16)


def _dsa(p, wuv_t, batch, seq):
    t = batch * seq
    tq = DSA_TQ
    nq = seq // tq
    topk = min(INDEX_TOPK, seq // 4)
    qcol = lambda n: pl.BlockSpec((n, tq), lambda b, i: (0, b * nq + i))
    return pl.pallas_call(
        functools.partial(_dsa_kernel, topk=float(topk), nq=nq),
        grid=(batch, nq),
        in_specs=[qcol(IDX_HEADS * IDX_DIM), qcol(16), qcol(DSA_HEADS * DSA_LATENT),
                  pl.BlockSpec((seq, IDX_DIM), lambda b, i: (b, 0)),
                  pl.BlockSpec((seq, DSA_LATENT), lambda b, i: (b, 0)),
                  pl.BlockSpec((CKVT_ROWS, seq), lambda b, i: (0, b)),
                  _full(wuv_t.shape)],
        out_specs=pl.BlockSpec((tq, DSA_WIDTH), lambda b, i: (b * nq + i, 0)),
        out_shape=jax.ShapeDtypeStruct((t, DSA_WIDTH), BF16),
        scratch_shapes=[pltpu.VMEM((IDX_DIM, IDX_HEADS * tq), BF16), pltpu.VMEM((DSA_LATENT, DSA_HEADS * tq), BF16),
                        pltpu.VMEM((seq, tq), F32), pltpu.VMEM((seq, tq), F32),
                        pltpu.VMEM((seq, DSA_HEADS * tq), F32), pltpu.VMEM((DSA_HEADS, CKVT_ROWS, tq), F32),
                        pltpu.VMEM((8, tq), I32), pltpu.VMEM((DSA_WIDTH, tq), F32)],
        compiler_params=_params("parallel", "arbitrary"),
        name="dsa",
    )(p["iqt"], p["wit"], p["dqt"], p["ki"], p["ckv"], p["ckvt"], wuv_t)


def _memkv_kernel(m_ref, g_ref, w_ref, o_ref):
    o_ref[...] = _dot(_rmsnorm(m_ref[...], g_ref[...]).astype(BF16), w_ref[...]).astype(BF16)


def _memkv(mem2d, g, w_kv, mem_len):
    rows, d = mem2d.shape
    return pl.pallas_call(
        _memkv_kernel,
        grid=(rows // mem_len,),
        in_specs=[pl.BlockSpec((mem_len, d), lambda b: (b, 0)), _full((1, d)), _full(w_kv.shape)],
        out_specs=pl.BlockSpec((mem_len, w_kv.shape[1]), lambda b: (b, 0)),
        out_shape=jax.ShapeDtypeStruct((rows, w_kv.shape[1]), BF16),
        compiler_params=_params("parallel"),
        name="memkv",
    )(mem2d, g, w_kv)


def _mixout_kernel(h_ref, yml_ref, ydsa_ref, woa_ref, wob_ref, g_ref, wq_ref, kv_ref, wo_ref, o_ref):
    d = h_ref.shape[1]
    hd = d // XA_HEADS
    h2 = h_ref[...] + _dot(yml_ref[...], woa_ref[...]) + _dot(ydsa_ref[...], wob_ref[...])
    q = _dot(_rmsnorm(h2, g_ref[...]).astype(BF16), wq_ref[...]).astype(BF16)
    heads = []
    for h in range(XA_HEADS):
        kh = kv_ref[:, h * hd:(h + 1) * hd]
        vh = kv_ref[:, d + h * hd:d + (h + 1) * hd]
        lg = _dot_nt(q[:, h * hd:(h + 1) * hd], kh) * (hd ** -0.5)
        p = jnp.exp(lg - jnp.max(lg, axis=1, keepdims=True))
        denom = jnp.sum(p, axis=1, keepdims=True)
        heads.append((_dot(p.astype(BF16), vh) / denom).astype(BF16))
    o_ref[...] = h2 + _dot(jnp.concatenate(heads, axis=1), wo_ref[...])


def _mixout(h1, yml, ydsa, w, kv, batch, seq, mem_len):
    t, d = h1.shape
    tm = MIX_TM
    nt = seq // tm
    row = lambda n: pl.BlockSpec((tm, n), lambda b, j: (b * nt + j, 0))
    return pl.pallas_call(
        _mixout_kernel,
        grid=(batch, nt),
        in_specs=[row(d), row(ML_WIDTH), row(DSA_WIDTH), _full(w["woa"].shape), _full(w["wob"].shape),
                  _full((1, d)), _full(w["wq"].shape),
                  pl.BlockSpec((mem_len, 2 * d), lambda b, j: (b, 0)), _full(w["wxo"].shape)],
        out_specs=row(d),
        out_shape=jax.ShapeDtypeStruct((t, d), F32),
        compiler_params=_params("parallel", "parallel"),
        name="mixout",
    )(h1, yml, ydsa, w["woa"], w["wob"], w["xg"], w["wq"], kv, w["wxo"])


def _layer(h, mem2d, p, batch, seq, mem_len, final_g):
    h1 = _ffn(h, p["f1g"], p["f1wg"], p["f1wu"], p["f1wd"], p["f1g"], final_norm=False)
    pr = _proj(h1, p["mixg"], p, seq)
    yml = _mlstm(pr, p["headg"], batch, seq)
    ydsa = _dsa(pr, p["wuv_t"], batch, seq)
    kv = _memkv(mem2d, p["memg"], p["wkv"], mem_len)
    h3 = _mixout(h1, yml, ydsa, p, kv, batch, seq, mem_len)
    fg = p["f2g"] if final_g is None else final_g
    return _ffn(h3, p["f2g"], p["f2wg"], p["f2wu"], p["f2wd"], fg, final_norm=final_g is not None)


def kernel(x, mem, ffn1_norm_g, ffn1_w_gate, ffn1_w_up, ffn1_w_down, mix_norm_g, w_in, mlstm_conv_w, mlstm_conv_b, mlstm_i_bias, mlstm_f_bias, mlstm_head_norm_g, dsa_kv_norm_g, idx_k_norm_g, dsa_w_uv, w_out, xattn_norm_g, mem_norm_g, xattn_w_q, xattn_w_kv, xattn_w_o, ffn2_norm_g, ffn2_w_gate, ffn2_w_up, ffn2_w_down, final_norm_g):
    batch, seq, d = x.shape
    mem_len = mem.shape[1]
    depth = w_in.shape[0]
    h = x.reshape(batch * seq, d)
    mem2d = mem.reshape(batch * mem_len, d)
    row = lambda a: a.reshape(1, -1).astype(F32)
    b16 = lambda a: a.astype(BF16)

    splits = (ML_WIDTH, ML_WIDTH, ML_WIDTH, ML_HEADS, ML_HEADS, ML_WIDTH, DSA_HEADS * DSA_LATENT, DSA_LATENT,
              IDX_HEADS * IDX_DIM, IDX_DIM, IDX_HEADS)
    offs = [0]
    for s in splits:
        offs.append(offs[-1] + s)

    for l in range(depth):
        wi = w_in[l]
        cols = [wi[:, offs[i]:offs[i + 1]] for i in range(len(splits))]
        mq, mk, mv, mi, mf, mo, dq, dc, iq, ik, iw = cols
        w_gate = jnp.concatenate([mi, mf], axis=1)
        b_gate = jnp.concatenate([mlstm_i_bias[l], mlstm_f_bias[l]])
        p = {
            "f1g": row(ffn1_norm_g[l]), "f1wg": b16(ffn1_w_gate[l]), "f1wu": b16(ffn1_w_up[l]), "f1wd": b16(ffn1_w_down[l]),
            "f2g": row(ffn2_norm_g[l]), "f2wg": b16(ffn2_w_gate[l]), "f2wu": b16(ffn2_w_up[l]), "f2wd": b16(ffn2_w_down[l]),
            "mixg": row(mix_norm_g[l]),
            "wqk": b16(jnp.concatenate([mq, mk], axis=1)), "wv": b16(mv), "wo": b16(mo),
            "wgc": b16(jnp.pad(w_gate, ((0, 0), (0, 128 - 2 * ML_HEADS)))),
            "wgr": b16(jnp.pad(w_gate.T, ((0, 16 - 2 * ML_HEADS), (0, 0)))),
            "bgc": jnp.pad(b_gate, (0, 128 - 2 * ML_HEADS)).reshape(1, 128).astype(F32),
            "bgr": jnp.pad(b_gate, (0, 16 - 2 * ML_HEADS)).reshape(16, 1).astype(F32),
            "cw": mlstm_conv_w[l].astype(F32), "cb": row(mlstm_conv_b[l]),
            "wdq_t": b16(dq.T), "wiq_t": b16(iq.T), "wiw_t": b16(jnp.pad(iw.T, ((0, 16 - IDX_HEADS), (0, 0)))),
            "wdc": b16(dc), "kvg": row(dsa_kv_norm_g[l]),
            "wik": b16(jnp.pad(ik, ((0, 0), (0, 128 - IDX_DIM)))), "ikg": row(idx_k_norm_g[l]),
            "headg": row(mlstm_head_norm_g[l]),
            "wuv_t": b16(jnp.swapaxes(dsa_w_uv[l], 1, 2)),
            "woa": b16(w_out[l][:ML_WIDTH]), "wob": b16(w_out[l][ML_WIDTH:]),
            "xg": row(xattn_norm_g[l]), "memg": row(mem_norm_g[l]),
            "wq": b16(xattn_w_q[l]), "wkv": b16(xattn_w_kv[l]), "wxo": b16(xattn_w_o[l]),
        }
        fg = row(final_norm_g) if l == depth - 1 else None
        h = _layer(h, mem2d, p, batch, seq, mem_len, fg)
    return h.reshape(batch, seq, d)
```

```python
import functools

import jax
import jax.numpy as jnp
from jax import lax
from jax.experimental import pallas as pl
from jax.experimental.pallas import tpu as pltpu

F32 = jnp.float32
BF16 = jnp.bfloat16
I32 = jnp.int32
EPS = 1e-6

ML_HEADS = 4
ML_HEAD_DIM = 128
ML_WIDTH = ML_HEADS * ML_HEAD_DIM
ML_CONV = 4
DSA_HEADS = 8
DSA_LATENT = 128
DSA_HEAD_DIM = 64
DSA_WIDTH = DSA_HEADS * DSA_HEAD_DIM
IDX_HEADS = 8
IDX_DIM = 64
INDEX_TOPK = 256
XA_HEADS = 4

FFN_TM = 512
FFN_FC = 256
PROJ_TM = 512
ML_CHUNK = 128
ML_SEQS = 2
DSA_TQ = 256
MIX_TM = 512

CKVT_ROWS = DSA_LATENT + 16

VMEM_LIMIT = 56 * 1024 * 1024
INT_MIN = -(2 ** 31)
NEG_INF = float("-inf")
LOG2E = 1.4426950408889634


def _rmsnorm(x, g):
    return x * lax.rsqrt(jnp.mean(x * x, axis=-1, keepdims=True) + EPS) * g


def _sigmoid(x):
    return 1.0 / (1.0 + jnp.exp(-x))


def _dot(a, b):
    return jnp.dot(a, b, preferred_element_type=F32)


def _dot_nt(a, b):
    return lax.dot_general(a, b, (((1,), (1,)), ((), ())), preferred_element_type=F32)


def _full(shape):
    return pl.BlockSpec(shape, lambda *_: (0,) * len(shape))


def _resident(shape):
    return pl.BlockSpec(shape, lambda *_: (0,) * len(shape), pipeline_mode=pl.Buffered(1))


def _params(*sem):
    return pltpu.CompilerParams(dimension_semantics=sem, vmem_limit_bytes=VMEM_LIMIT)


def _ffn_kernel(x_ref, g_ref, wg_ref, wu_ref, wd_ref, fg_ref, o_ref, *, final_norm):
    x = x_ref[...]
    xn = _rmsnorm(x, g_ref[...]).astype(BF16)
    d_ff = wg_ref.shape[1]
    acc = jnp.zeros(x.shape, F32)
    for c in range(d_ff // FFN_FC):
        sl = slice(c * FFN_FC, (c + 1) * FFN_FC)
        gate = _dot(xn, wg_ref[:, sl].astype(BF16))
        up = _dot(xn, wu_ref[:, sl].astype(BF16))
        act = (gate * _sigmoid(gate) * up).astype(BF16)
        acc = acc + _dot(act, wd_ref[sl, :].astype(BF16))
    y = x + 0.5 * acc
    if final_norm:
        y = _rmsnorm(y, fg_ref[...])
    o_ref[...] = y


def _ffn(x, g, wg, wu, wd, fg, final_norm):
    t, d = x.shape
    d_ff = wg.shape[1]
    return pl.pallas_call(
        functools.partial(_ffn_kernel, final_norm=final_norm),
        grid=(t // FFN_TM,),
        in_specs=[
            pl.BlockSpec((FFN_TM, d), lambda i: (i, 0)),
            _full((1, d)),
            _resident((d, d_ff)),
            _resident((d, d_ff)),
            _resident((d_ff, d)),
            _full((1, d)),
        ],
        out_specs=pl.BlockSpec((FFN_TM, d), lambda i: (i, 0)),
        out_shape=jax.ShapeDtypeStruct((t, d), F32),
        compiler_params=_params("parallel"),
        name="ffn_final" if final_norm else "ffn",
    )(x, g, wg, wu, wd, fg)


def _proj_kernel(h_ref, g_ref, wqk_ref, wv_ref, wo_ref, wgc_ref, wgr_ref, bgc_ref, bgr_ref,
                 cw_ref, cb_ref, wdq_ref, wiq_ref, wiw_ref, wdc_ref, kvg_ref, wik_ref, ikg_ref,
                 q_ref, k_ref, v_ref, o_ref, gc_ref, gr_ref, dqt_ref, iqt_ref, wit_ref,
                 ckv_ref, ckvt_ref, ki_ref, zbuf, *, tiles_per_seq):
    tm = h_ref.shape[0]
    u = _rmsnorm(h_ref[...], g_ref[...]).astype(BF16)

    @pl.when(pl.program_id(0) % tiles_per_seq == 0)
    def _():
        zbuf[0:8, :] = jnp.zeros((8, zbuf.shape[1]), F32)

    zbuf[8:8 + tm, :] = _dot(u, wqk_ref[...])
    y = jnp.zeros((tm, zbuf.shape[1]), F32) + cb_ref[...]
    for j in range(ML_CONV):
        y = y + zbuf[5 + j:5 + j + tm, :] * cw_ref[j:j + 1, :]
    zbuf[0:8, :] = zbuf[tm:tm + 8, :]
    y = y * _sigmoid(y)
    q_ref[...] = y[:, :ML_WIDTH].astype(BF16)
    k_ref[...] = (y[:, ML_WIDTH:] * (ML_HEAD_DIM ** -0.5)).astype(BF16)
    v_ref[...] = _dot(u, wv_ref[...]).astype(BF16)
    o_ref[...] = _dot(u, wo_ref[...]).astype(BF16)

    def gates(z, idx):
        return jnp.where(idx < ML_HEADS, z, jnp.minimum(z, 0.0) - jnp.log(1.0 + jnp.exp(-jnp.abs(z))))

    zc = _dot(u, wgc_ref[...]) + bgc_ref[...]
    gc_ref[...] = gates(zc, lax.broadcasted_iota(I32, zc.shape, 1))
    zr = _dot_nt(wgr_ref[...], u) + bgr_ref[...]
    gr_ref[...] = gates(zr, lax.broadcasted_iota(I32, zr.shape, 0))

    dqt_ref[...] = (_dot_nt(wdq_ref[...], u) * (DSA_LATENT ** -0.5 * LOG2E)).astype(BF16)
    iqt_ref[...] = _dot_nt(wiq_ref[...], u).astype(BF16)
    wit_ref[...] = _dot_nt(wiw_ref[...], u) * (IDX_HEADS ** -0.5) * (IDX_DIM ** -0.5)
    ckv = _rmsnorm(_dot(u, wdc_ref[...]), kvg_ref[...])
    ckv_ref[...] = ckv.astype(BF16)
    ckvt_ref[0:DSA_LATENT, :] = ckv.T.astype(BF16)
    ckvt_ref[DSA_LATENT:, :] = (lax.broadcasted_iota(I32, (CKVT_ROWS - DSA_LATENT, tm), 0) == 0).astype(BF16)
    ik = _dot(u, wik_ref[...])[:, :IDX_DIM]
    ki_ref[...] = _rmsnorm(ik, ikg_ref[...]).astype(BF16)


def _proj(h1, g, w, seq):
    t, d = h1.shape
    tm = PROJ_TM
    row = lambda n: pl.BlockSpec((tm, n), lambda i: (i, 0))
    col = lambda n: pl.BlockSpec((n, tm), lambda i: (0, i))
    outs = [
        ("q", row(ML_WIDTH), (t, ML_WIDTH), BF16),
        ("k", row(ML_WIDTH), (t, ML_WIDTH), BF16),
        ("v", row(ML_WIDTH), (t, ML_WIDTH), BF16),
        ("o", row(ML_WIDTH), (t, ML_WIDTH), BF16),
        ("gc", row(128), (t, 128), F32),
        ("gr", col(16), (16, t), F32),
        ("dqt", col(DSA_HEADS * DSA_LATENT), (DSA_HEADS * DSA_LATENT, t), BF16),
        ("iqt", col(IDX_HEADS * IDX_DIM), (IDX_HEADS * IDX_DIM, t), BF16),
        ("wit", col(16), (16, t), F32),
        ("ckv", row(DSA_LATENT), (t, DSA_LATENT), BF16),
        ("ckvt", col(CKVT_ROWS), (CKVT_ROWS, t), BF16),
        ("ki", row(IDX_DIM), (t, IDX_DIM), BF16),
    ]
    ins = [h1, g, w["wqk"], w["wv"], w["wo"], w["wgc"], w["wgr"], w["bgc"], w["bgr"], w["cw"], w["cb"],
           w["wdq_t"], w["wiq_t"], w["wiw_t"], w["wdc"], w["kvg"], w["wik"], w["ikg"]]
    res = pl.pallas_call(
        functools.partial(_proj_kernel, tiles_per_seq=seq // tm),
        grid=(t // tm,),
        in_specs=[row(d)] + [_full(a.shape) for a in ins[1:]],
        out_specs=[o[1] for o in outs],
        out_shape=[jax.ShapeDtypeStruct(o[2], o[3]) for o in outs],
        scratch_shapes=[pltpu.VMEM((tm + 8, 2 * ML_WIDTH), F32)],
        compiler_params=_params("arbitrary"),
        name="proj",
    )(*ins)
    return {o[0]: r for o, r in zip(outs, res)}


def _split3(x):
    hi = x.astype(BF16)
    r1 = x - hi.astype(F32)
    mid = r1.astype(BF16)
    lo = (r1 - mid.astype(F32)).astype(BF16)
    return hi, mid, lo


def _mlstm_kernel(q_ref, k_ref, v_ref, o_ref, gc_ref, gr_ref, hg_ref, y_ref, c_ref, m_ref, *, seq):
    L = ML_CHUNK
    d = ML_HEAD_DIM
    c_ref[...] = jnp.zeros(c_ref.shape, F32)
    m_ref[...] = jnp.zeros(m_ref.shape, F32)

    ri = lax.broadcasted_iota(I32, (L, L), 0)
    ci = lax.broadcasted_iota(I32, (L, L), 1)
    causal = ci <= ri
    tri_lo = causal.astype(BF16)
    tri_up = (ri <= ci).astype(BF16)
    ones_col = (lax.broadcasted_iota(I32, (L, d), 1) == 0).astype(F32)

    nch = ML_SEQS * ML_HEADS
    causal_all = jnp.concatenate([causal] * nch, axis=0)
    stack = lambda parts: jnp.concatenate(parts, axis=0)
    col_of = lambda x4: stack([jnp.broadcast_to(x4[j:j + 1, :], (L, 1)) for j in range(nch)])
    gain = stack([jnp.broadcast_to(hg_ref[:, (j % ML_HEADS) * d:(j % ML_HEADS + 1) * d], (L, d)) for j in range(nch)])

    def chunk(c, carry):
        r0s = [pl.multiple_of(sq * seq + c * L, L) for sq in range(ML_SEQS)]
        hsl = lambda j: slice((j % ML_HEADS) * d, (j % ML_HEADS + 1) * d)
        rws = lambda j: pl.ds(r0s[j // ML_HEADS], L)
        bcs, brs, gcs, grs = [], [], [], []
        for sq in range(ML_SEQS):
            gc = gc_ref[pl.ds(r0s[sq], L), :]
            gr = gr_ref[:, pl.ds(r0s[sq], L)]
            bcs.append(sum(_dot(tri_lo, p) for p in _split3(gc)))
            brs.append(sum(_dot(p, tri_up) for p in _split3(gr)))
            gcs.append(gc)
            grs.append(gr)
        b_col = stack([bcs[j // ML_HEADS][:, ML_HEADS + j % ML_HEADS:ML_HEADS + j % ML_HEADS + 1] for j in range(nch)])
        i_col = stack([gcs[j // ML_HEADS][:, j % ML_HEADS:j % ML_HEADS + 1] for j in range(nch)])
        b_row4 = stack([brs[sq][ML_HEADS:2 * ML_HEADS, :] for sq in range(ML_SEQS)])
        i_row4 = stack([grs[sq][0:ML_HEADS, :] for sq in range(ML_SEQS)])
        u_row4 = i_row4 - b_row4
        u_blk = stack([jnp.broadcast_to(u_row4[j:j + 1, :], (L, L)) for j in range(nch)])
        m_prev4 = m_ref[0:nch, 0:1]
        logw = jnp.where(causal_all, b_col + u_blk, NEG_INF)
        inter = b_col + col_of(m_prev4)
        mj = jnp.maximum(inter, jnp.max(logw, axis=1, keepdims=True))
        w = jnp.exp(logw - mj)
        a = jnp.exp(inter - mj)
        qs = [q_ref[rws(j), hsl(j)] for j in range(nch)]
        ks = [k_ref[rws(j), hsl(j)] for j in range(nch)]
        vaugs = [jnp.concatenate([v_ref[rws(j), hsl(j)].astype(F32), ones_col], axis=1) for j in range(nch)]
        sqk = (stack([_dot_nt(qs[j], ks[j]) for j in range(nch)]) * w).astype(BF16)
        cts = [c_ref[j] for j in range(nch)]
        tot = a * stack([_dot(qs[j], cts[j].astype(BF16)) for j in range(nch)]) + stack(
            [_dot(sqk[j * L:(j + 1) * L, :], vaugs[j].astype(BF16)) for j in range(nch)])
        hh = tot[:, :d] / jnp.maximum(jnp.abs(tot[:, d:d + 1]), jnp.exp(-mj))
        gate = _sigmoid(stack([o_ref[rws(j), hsl(j)] for j in range(nch)]).astype(F32))
        y = (_rmsnorm(hh, gain) * gate).astype(BF16)
        for j in range(nch):
            y_ref[rws(j), hsl(j)] = y[j * L:(j + 1) * L, :]
        b_last4 = b_row4[:, L - 1:L]
        g_row4 = b_last4 + u_row4
        m_new4 = jnp.maximum(b_last4 + m_prev4, jnp.max(g_row4, axis=1, keepdims=True))
        dec4 = jnp.exp(b_last4 + m_prev4 - m_new4)
        wg = jnp.exp(col_of(b_last4 - m_new4) - b_col + i_col)
        for j in range(nch):
            kt = ks[j].astype(F32).T.astype(BF16)
            wv = (wg[j * L:(j + 1) * L, :] * vaugs[j]).astype(BF16)
            c_ref[j] = dec4[j:j + 1, :] * cts[j] + _dot(kt, wv)
        m_ref[0:nch, :] = jnp.broadcast_to(m_new4, (nch, m_ref.shape[1]))
        return carry

    lax.fori_loop(0, seq // L, chunk, 0)


def _mlstm(p, head_g, batch, seq):
    t = batch * seq
    rows = ML_SEQS * seq
    row = lambda n: pl.BlockSpec((rows, n), lambda b: (b, 0))
    return pl.pallas_call(
        functools.partial(_mlstm_kernel, seq=seq),
        grid=(batch // ML_SEQS,),
        in_specs=[row(ML_WIDTH), row(ML_WIDTH), row(ML_WIDTH), row(ML_WIDTH), row(128),
                  pl.BlockSpec((16, rows), lambda b: (0, b)), _full((1, ML_WIDTH))],
        out_specs=row(ML_WIDTH),
        out_shape=jax.ShapeDtypeStruct((t, ML_WIDTH), BF16),
        scratch_shapes=[pltpu.VMEM((ML_SEQS * ML_HEADS, ML_HEAD_DIM, 2 * ML_HEAD_DIM), F32),
                        pltpu.VMEM((ML_SEQS * ML_HEADS, 128), F32)],
        compiler_params=_params("parallel"),
        name="mlstm",
    )(p["q"], p["k"], p["v"], p["o"], p["gc"], p["gr"], head_g)


def _float_of_code(code):
    key = code ^ jnp.int32(INT_MIN)
    bits = jnp.where(key < 0, key ^ jnp.int32(0x7FFFFFFF), key)
    return lax.bitcast_convert_type(bits, F32)


def _colmax8(x):
    parts = [x[8 * j:8 * j + 8, :] for j in range(x.shape[0] // 8)]
    while len(parts) > 1:
        parts = [jnp.maximum(parts[j], parts[j + 1]) for j in range(0, len(parts) - 1, 2)] + (
            [parts[-1]] if len(parts) % 2 else [])
    return parts[0]


def _colsum(x):
    groups = x.shape[0] // 8
    lanes = min(2, groups)
    parts = [x[8 * j:8 * j + 8, :] for j in range(lanes)]
    for j in range(lanes, groups):
        parts[j % lanes] = parts[j % lanes] + x[8 * j:8 * j + 8, :]
    while len(parts) > 1:
        parts = [parts[j] + parts[j + 1] for j in range(0, len(parts), 2)]
    return jnp.sum(parts[0], axis=0, keepdims=True)


def _dsa_kernel(iqt_ref, wit_ref, dqt_ref, ki_ref, ckv_ref, ckvt_ref, wuvt_ref, y_ref,
                iqa_ref, dqa_ref, score_ref, bias_ref, lg_ref, acc_ref, code_ref, yt_ref, m8_ref, *, topk, nq):
    tq = iqt_ref.shape[1]
    nh = DSA_HEADS
    nk = pl.program_id(1) + 1
    last = nk - 1
    rows = lambda c: pl.ds(pl.multiple_of(c * tq, tq), tq)
    hs = lambda h: slice(h * tq, (h + 1) * tq)
    dmat = lax.broadcasted_iota(I32, (tq, tq), 0) - lax.broadcasted_iota(I32, (tq, tq), 1)
    allowed = lambda c: dmat <= jnp.where(c < last, tq, 0)

    for h in range(nh):
        iqa_ref[:, hs(h)] = iqt_ref[h * IDX_DIM:(h + 1) * IDX_DIM, :]
        dqa_ref[:, hs(h)] = dqt_ref[h * DSA_LATENT:(h + 1) * DSA_LATENT, :]
    wi_all = jnp.concatenate([wit_ref[h:h + 1, :] for h in range(IDX_HEADS)], axis=1)

    def for_blocks(body):
        def quad(p, carry):
            for u in range(4):
                body(4 * p + u)
            return carry

        def pair():
            body(nk & ~3)
            body((nk & ~3) + 1)

        lax.fori_loop(0, lax.shift_right_logical(nk, 2), quad, 0)
        pl.when((nk & 2) != 0)(pair)
        pl.when((nk & 1) != 0)(lambda: body(last))

    def idx_block(c):
        w = wi_all * jnp.maximum(_dot(ki_ref[rows(c), :], iqa_ref[...]), 0.0)
        acc = w[:, hs(0)]
        for h in range(1, IDX_HEADS):
            acc = acc + w[:, hs(h)]
        score_ref[rows(c), :] = jnp.where(allowed(c), acc, NEG_INF)

    for_blocks(idx_block)

    def search(n):
        def bit_pass(i, code):
            cand_code = code | lax.shift_left(jnp.int32(1), 31 - i)
            cand = _float_of_code(cand_code)
            cnt = _colsum(jnp.where(score_ref[0:n * tq, :] < cand, 0.0, 1.0))
            return jnp.where(cnt >= topk, cand_code, code)

        code_ref[0:1, :] = lax.fori_loop(0, 32, bit_pass, jnp.zeros((1, tq), I32))

    for n in range(1, nq + 1):
        pl.when(nk == n)(functools.partial(search, n))
    thr = _float_of_code(code_ref[0:1, :])

    def mask_block(c, carry):
        n_gt, n_eq = carry
        s = score_ref[rows(c), :]
        bias_ref[rows(c), :] = jnp.where((s >= thr) & allowed(c), 0.0, NEG_INF)
        n_gt = n_gt + _colsum(jnp.where(s > thr, 1.0, 0.0))
        n_eq = n_eq + _colsum(jnp.where(s == thr, 1.0, 0.0))
        return n_gt, n_eq

    zero = jnp.zeros((1, tq), F32)
    n_gt, n_eq = lax.fori_loop(0, nk, mask_block, (zero, zero))
    room = topk - n_gt

    @pl.when(jnp.max(n_eq - room) > 0.0)
    def _():
        lo = (dmat > 0).astype(BF16)

        def tie_block(c, before):
            s = score_ref[rows(c), :]
            ef = jnp.where(s == thr, 1.0, 0.0)
            rank = before + _dot(lo, ef.astype(BF16))
            keep = ((s > thr) | ((s == thr) & (rank < room))) & allowed(c)
            bias_ref[rows(c), :] = jnp.where(keep, 0.0, NEG_INF)
            return before + jnp.sum(ef, axis=0, keepdims=True)

        lax.fori_loop(0, nk, tie_block, zero)

    m8_ref[...] = jnp.full(m8_ref.shape, NEG_INF, F32)

    def logit_block(c):
        s_all = _dot(ckv_ref[rows(c), :], dqa_ref[...])
        b = bias_ref[rows(c), :]
        for h in range(nh):
            s = s_all[:, hs(h)] + b
            lg_ref[rows(c), hs(h)] = s
            m8_ref[:, hs(h)] = jnp.maximum(m8_ref[:, hs(h)], _colmax8(s))

    for_blocks(logit_block)
    m = jnp.max(m8_ref[...], axis=0, keepdims=True)
    acc_ref[...] = jnp.zeros(acc_ref.shape, F32)

    def prob_block(c):
        ckvt = ckvt_ref[:, rows(c)]
        for h in range(nh):
            p = jnp.exp2(lg_ref[rows(c), hs(h)] - m[:, hs(h)]).astype(BF16)
            acc_ref[h] += _dot(ckvt, p)

    for_blocks(prob_block)
    for h in range(nh):
        ot = acc_ref[h, 0:DSA_LATENT, :] / acc_ref[h, DSA_LATENT:DSA_LATENT + 1, :]
        yt_ref[h * DSA_HEAD_DIM:(h + 1) * DSA_HEAD_DIM, :] = _dot(wuvt_ref[h], ot.astype(BF16))
    y_ref[...] = yt_ref[...].T.astype(BF16)


def _dsa(p, wuv_t, batch, seq):
    t = batch * seq
    tq = DSA_TQ
    nq = seq // tq
    topk = min(INDEX_TOPK, seq // 4)
    qcol = lambda n: pl.BlockSpec((n, tq), lambda b, i: (0, b * nq + i))
    return pl.pallas_call(
        functools.partial(_dsa_kernel, topk=float(topk), nq=nq),
        grid=(batch, nq),
        in_specs=[qcol(IDX_HEADS * IDX_DIM), qcol(16), qcol(DSA_HEADS * DSA_LATENT),
                  pl.BlockSpec((seq, IDX_DIM), lambda b, i: (b, 0)),
                  pl.BlockSpec((seq, DSA_LATENT), lambda b, i: (b, 0)),
                  pl.BlockSpec((CKVT_ROWS, seq), lambda b, i: (0, b)),
                  _full(wuv_t.shape)],
        out_specs=pl.BlockSpec((tq, DSA_WIDTH), lambda b, i: (b * nq + i, 0)),
        out_shape=jax.ShapeDtypeStruct((t, DSA_WIDTH), BF16),
        scratch_shapes=[pltpu.VMEM((IDX_DIM, IDX_HEADS * tq), BF16), pltpu.VMEM((DSA_LATENT, DSA_HEADS * tq), BF16),
                        pltpu.VMEM((seq, tq), F32), pltpu.VMEM((seq, tq), F32),
                        pltpu.VMEM((seq, DSA_HEADS * tq), F32), pltpu.VMEM((DSA_HEADS, CKVT_ROWS, tq), F32),
                        pltpu.VMEM((8, tq), I32), pltpu.VMEM((DSA_WIDTH, tq), F32),
                        pltpu.VMEM((8, DSA_HEADS * tq), F32)],
        compiler_params=_params("parallel", "arbitrary"),
        name="dsa",
    )(p["iqt"], p["wit"], p["dqt"], p["ki"], p["ckv"], p["ckvt"], wuv_t)


def _memkv_kernel(m_ref, g_ref, w_ref, o_ref):
    o_ref[...] = _dot(_rmsnorm(m_ref[...], g_ref[...]).astype(BF16), w_ref[...].astype(BF16)).astype(BF16)


def _memkv(mem2d, g, w_kv, mem_len):
    rows, d = mem2d.shape
    return pl.pallas_call(
        _memkv_kernel,
        grid=(rows // mem_len,),
        in_specs=[pl.BlockSpec((mem_len, d), lambda b: (b, 0)), _full((1, d)), _resident(w_kv.shape)],
        out_specs=pl.BlockSpec((mem_len, w_kv.shape[1]), lambda b: (b, 0)),
        out_shape=jax.ShapeDtypeStruct((rows, w_kv.shape[1]), BF16),
        compiler_params=_params("parallel"),
        name="memkv",
    )(mem2d, g, w_kv)


def _mixout_kernel(h_ref, yml_ref, ydsa_ref, woa_ref, wob_ref, g_ref, wq_ref, kv_ref, wo_ref, o_ref):
    d = h_ref.shape[1]
    hd = d // XA_HEADS
    h2 = h_ref[...] + _dot(yml_ref[...], woa_ref[...].astype(BF16)) + _dot(ydsa_ref[...], wob_ref[...].astype(BF16))
    q = _dot(_rmsnorm(h2, g_ref[...]).astype(BF16), wq_ref[...].astype(BF16)).astype(BF16)
    heads = []
    for h in range(XA_HEADS):
        kh = kv_ref[:, h * hd:(h + 1) * hd]
        vh = kv_ref[:, d + h * hd:d + (h + 1) * hd]
        lg = _dot_nt(q[:, h * hd:(h + 1) * hd], kh) * (hd ** -0.5)
        p = jnp.exp(lg - jnp.max(lg, axis=1, keepdims=True))
        denom = jnp.sum(p, axis=1, keepdims=True)
        heads.append((_dot(p.astype(BF16), vh) / denom).astype(BF16))
    o_ref[...] = h2 + _dot(jnp.concatenate(heads, axis=1), wo_ref[...].astype(BF16))


def _mixout(h1, yml, ydsa, w, kv, batch, seq, mem_len):
    t, d = h1.shape
    tm = MIX_TM
    nt = seq // tm
    row = lambda n: pl.BlockSpec((tm, n), lambda b, j: (b * nt + j, 0))
    return pl.pallas_call(
        _mixout_kernel,
        grid=(batch, nt),
        in_specs=[row(d), row(ML_WIDTH), row(DSA_WIDTH), _resident(w["woa"].shape), _resident(w["wob"].shape),
                  _full((1, d)), _resident(w["wq"].shape),
                  pl.BlockSpec((mem_len, 2 * d), lambda b, j: (b, 0)), _resident(w["wxo"].shape)],
        out_specs=row(d),
        out_shape=jax.ShapeDtypeStruct((t, d), F32),
        compiler_params=_params("parallel", "parallel"),
        name="mixout",
    )(h1, yml, ydsa, w["woa"], w["wob"], w["xg"], w["wq"], kv, w["wxo"])


def _layer(h, mem2d, p, batch, seq, mem_len, final_g):
    h1 = _ffn(h, p["f1g"], p["f1wg"], p["f1wu"], p["f1wd"], p["f1g"], final_norm=False)
    pr = _proj(h1, p["mixg"], p, seq)
    yml = _mlstm(pr, p["headg"], batch, seq)
    ydsa = _dsa(pr, p["wuv_t"], batch, seq)
    kv = _memkv(mem2d, p["memg"], p["wkv"], mem_len)
    h3 = _mixout(h1, yml, ydsa, p, kv, batch, seq, mem_len)
    fg = p["f2g"] if final_g is None else final_g
    return _ffn(h3, p["f2g"], p["f2wg"], p["f2wu"], p["f2wd"], fg, final_norm=final_g is not None)


def kernel(x, mem, ffn1_norm_g, ffn1_w_gate, ffn1_w_up, ffn1_w_down, mix_norm_g, w_in, mlstm_conv_w, mlstm_conv_b, mlstm_i_bias, mlstm_f_bias, mlstm_head_norm_g, dsa_kv_norm_g, idx_k_norm_g, dsa_w_uv, w_out, xattn_norm_g, mem_norm_g, xattn_w_q, xattn_w_kv, xattn_w_o, ffn2_norm_g, ffn2_w_gate, ffn2_w_up, ffn2_w_down, final_norm_g):
    batch, seq, d = x.shape
    mem_len = mem.shape[1]
    depth = w_in.shape[0]
    h = x.reshape(batch * seq, d)
    mem2d = mem.reshape(batch * mem_len, d)
    row = lambda a: a.reshape(1, -1).astype(F32)
    b16 = lambda a: a.astype(BF16)

    splits = (ML_WIDTH, ML_WIDTH, ML_WIDTH, ML_HEADS, ML_HEADS, ML_WIDTH, DSA_HEADS * DSA_LATENT, DSA_LATENT,
              IDX_HEADS * IDX_DIM, IDX_DIM, IDX_HEADS)
    offs = [0]
    for s in splits:
        offs.append(offs[-1] + s)

    for l in range(depth):
        wi = w_in[l]
        cols = [wi[:, offs[i]:offs[i + 1]] for i in range(len(splits))]
        mq, mk, mv, mi, mf, mo, dq, dc, iq, ik, iw = cols
        w_gate = jnp.concatenate([mi, mf], axis=1)
        b_gate = jnp.concatenate([mlstm_i_bias[l], mlstm_f_bias[l]])
        p = {
            "f1g": row(ffn1_norm_g[l]), "f1wg": ffn1_w_gate[l], "f1wu": ffn1_w_up[l], "f1wd": ffn1_w_down[l],
            "f2g": row(ffn2_norm_g[l]), "f2wg": ffn2_w_gate[l], "f2wu": ffn2_w_up[l], "f2wd": ffn2_w_down[l],
            "mixg": row(mix_norm_g[l]),
            "wqk": b16(jnp.concatenate([mq, mk], axis=1)), "wv": b16(mv), "wo": b16(mo),
            "wgc": b16(jnp.pad(w_gate, ((0, 0), (0, 128 - 2 * ML_HEADS)))),
            "wgr": b16(jnp.pad(w_gate.T, ((0, 16 - 2 * ML_HEADS), (0, 0)))),
            "bgc": jnp.pad(b_gate, (0, 128 - 2 * ML_HEADS)).reshape(1, 128).astype(F32),
            "bgr": jnp.pad(b_gate, (0, 16 - 2 * ML_HEADS)).reshape(16, 1).astype(F32),
            "cw": mlstm_conv_w[l].astype(F32), "cb": row(mlstm_conv_b[l]),
            "wdq_t": b16(dq.T), "wiq_t": b16(iq.T), "wiw_t": b16(jnp.pad(iw.T, ((0, 16 - IDX_HEADS), (0, 0)))),
            "wdc": b16(dc), "kvg": row(dsa_kv_norm_g[l]),
            "wik": b16(jnp.pad(ik, ((0, 0), (0, 128 - IDX_DIM)))), "ikg": row(idx_k_norm_g[l]),
            "headg": row(mlstm_head_norm_g[l]),
            "wuv_t": b16(jnp.swapaxes(dsa_w_uv[l], 1, 2)),
            "woa": w_out[l][:ML_WIDTH], "wob": w_out[l][ML_WIDTH:],
            "xg": row(xattn_norm_g[l]), "memg": row(mem_norm_g[l]),
            "wq": xattn_w_q[l], "wkv": xattn_w_kv[l], "wxo": xattn_w_o[l],
        }
        fg = row(final_norm_g) if l == depth - 1 else None
        h = _layer(h, mem2d, p, batch, seq, mem_len, fg)
    return h.reshape(batch, seq, d)
```

```python
import functools

import jax
import jax.numpy as jnp
from jax import lax
from jax.experimental import pallas as pl
from jax.experimental.pallas import tpu as pltpu

F32 = jnp.float32
BF16 = jnp.bfloat16
I32 = jnp.int32
EPS = 1e-6

ML_HEADS = 4
ML_HEAD_DIM = 128
ML_WIDTH = ML_HEADS * ML_HEAD_DIM
ML_CONV = 4
DSA_HEADS = 8
DSA_LATENT = 128
DSA_HEAD_DIM = 64
DSA_WIDTH = DSA_HEADS * DSA_HEAD_DIM
IDX_HEADS = 8
IDX_DIM = 64
INDEX_TOPK = 256
XA_HEADS = 4

FFN_TM = 512
FFN_FC = 256
PROJ_TM = 1024
ML_CHUNK = 128
ML_SEQS = 2
DSA_TQ = 256
MIX_TM = 512

CKVT_ROWS = DSA_LATENT + 16

VMEM_LIMIT = 56 * 1024 * 1024
INT_MIN = -(2 ** 31)
NEG_INF = float("-inf")
CODE_NEG_INF = 0x007FFFFF
LOG2E = 1.4426950408889634


def _rmsnorm(x, g):
    return x * lax.rsqrt(jnp.mean(x * x, axis=-1, keepdims=True) + EPS) * g


def _sigmoid(x):
    return 1.0 / (1.0 + jnp.exp(-x))


def _dot(a, b):
    return jnp.dot(a, b, preferred_element_type=F32)


def _dot_nt(a, b):
    return lax.dot_general(a, b, (((1,), (1,)), ((), ())), preferred_element_type=F32)


def _full(shape):
    return pl.BlockSpec(shape, lambda *_: (0,) * len(shape))


def _resident(shape):
    return pl.BlockSpec(shape, lambda *_: (0,) * len(shape), pipeline_mode=pl.Buffered(1))


def _params(*sem):
    return pltpu.CompilerParams(dimension_semantics=sem, vmem_limit_bytes=VMEM_LIMIT)


def _ffn_kernel(x_ref, g_ref, wg_ref, wu_ref, wd_ref, fg_ref, o_ref, *, final_norm):
    x = x_ref[...]
    xn = _rmsnorm(x, g_ref[...]).astype(BF16)
    d_ff = wg_ref.shape[1]
    acc = jnp.zeros(x.shape, F32)
    for c in range(d_ff // FFN_FC):
        sl = slice(c * FFN_FC, (c + 1) * FFN_FC)
        gate = _dot(xn, wg_ref[:, sl].astype(BF16))
        up = _dot(xn, wu_ref[:, sl].astype(BF16))
        act = (gate * _sigmoid(gate) * up).astype(BF16)
        acc = acc + _dot(act, wd_ref[sl, :].astype(BF16))
    y = x + 0.5 * acc
    if final_norm:
        y = _rmsnorm(y, fg_ref[...])
    o_ref[...] = y


def _ffn(x, g, wg, wu, wd, fg, final_norm):
    t, d = x.shape
    d_ff = wg.shape[1]
    return pl.pallas_call(
        functools.partial(_ffn_kernel, final_norm=final_norm),
        grid=(t // FFN_TM,),
        in_specs=[
            pl.BlockSpec((FFN_TM, d), lambda i: (i, 0)),
            _full((1, d)),
            _resident((d, d_ff)),
            _resident((d, d_ff)),
            _resident((d_ff, d)),
            _full((1, d)),
        ],
        out_specs=pl.BlockSpec((FFN_TM, d), lambda i: (i, 0)),
        out_shape=jax.ShapeDtypeStruct((t, d), F32),
        compiler_params=_params("parallel"),
        name="ffn_final" if final_norm else "ffn",
    )(x, g, wg, wu, wd, fg)


def _proj_kernel(h_ref, g_ref, wqk_ref, wv_ref, wo_ref, wgc_ref, wgr_ref, bgc_ref, bgr_ref,
                 cw_ref, cb_ref, wdq_ref, wiq_ref, wiw_ref, wdc_ref, kvg_ref, wik_ref, ikg_ref,
                 q_ref, k_ref, v_ref, o_ref, gc_ref, gr_ref, dqt_ref, iqt_ref, wit_ref,
                 ckv_ref, ckvt_ref, ki_ref, zbuf, *, tiles_per_seq):
    tm = h_ref.shape[0]
    u = _rmsnorm(h_ref[...], g_ref[...]).astype(BF16)

    @pl.when(pl.program_id(0) % tiles_per_seq == 0)
    def _():
        zbuf[0:8, :] = jnp.zeros((8, zbuf.shape[1]), F32)

    zbuf[8:8 + tm, :] = _dot(u, wqk_ref[...])
    y = jnp.zeros((tm, zbuf.shape[1]), F32) + cb_ref[...]
    for j in range(ML_CONV):
        y = y + zbuf[5 + j:5 + j + tm, :] * cw_ref[j:j + 1, :]
    zbuf[0:8, :] = zbuf[tm:tm + 8, :]
    y = y * _sigmoid(y)
    q_ref[...] = y[:, :ML_WIDTH].astype(BF16)
    k_ref[...] = (y[:, ML_WIDTH:] * (ML_HEAD_DIM ** -0.5)).astype(BF16)
    v_ref[...] = _dot(u, wv_ref[...]).astype(BF16)
    o_ref[...] = _dot(u, wo_ref[...]).astype(BF16)

    def gates(z, idx):
        return jnp.where(idx < ML_HEADS, z, jnp.minimum(z, 0.0) - jnp.log(1.0 + jnp.exp(-jnp.abs(z))))

    zc = _dot(u, wgc_ref[...]) + bgc_ref[...]
    gc_ref[...] = gates(zc, lax.broadcasted_iota(I32, zc.shape, 1))
    zr = _dot_nt(wgr_ref[...], u) + bgr_ref[...]
    gr_ref[...] = gates(zr, lax.broadcasted_iota(I32, zr.shape, 0))

    dqt_ref[...] = (_dot_nt(wdq_ref[...], u) * (DSA_LATENT ** -0.5 * LOG2E)).astype(BF16)
    iqt_ref[...] = _dot_nt(wiq_ref[...], u).astype(BF16)
    wit_ref[...] = _dot_nt(wiw_ref[...], u) * (IDX_HEADS ** -0.5) * (IDX_DIM ** -0.5)
    ckv = _rmsnorm(_dot(u, wdc_ref[...]), kvg_ref[...])
    ckv_ref[...] = ckv.astype(BF16)
    ckvt_ref[0:DSA_LATENT, :] = ckv.T.astype(BF16)
    ckvt_ref[DSA_LATENT:, :] = (lax.broadcasted_iota(I32, (CKVT_ROWS - DSA_LATENT, tm), 0) == 0).astype(BF16)
    ik = _dot(u, wik_ref[...])[:, :IDX_DIM]
    ki_ref[...] = _rmsnorm(ik, ikg_ref[...]).astype(BF16)


def _proj(h1, g, w, seq):
    t, d = h1.shape
    tm = PROJ_TM
    row = lambda n: pl.BlockSpec((tm, n), lambda i: (i, 0))
    col = lambda n: pl.BlockSpec((n, tm), lambda i: (0, i))
    outs = [
        ("q", row(ML_WIDTH), (t, ML_WIDTH), BF16),
        ("k", row(ML_WIDTH), (t, ML_WIDTH), BF16),
        ("v", row(ML_WIDTH), (t, ML_WIDTH), BF16),
        ("o", row(ML_WIDTH), (t, ML_WIDTH), BF16),
        ("gc", row(128), (t, 128), F32),
        ("gr", col(16), (16, t), F32),
        ("dqt", col(DSA_HEADS * DSA_LATENT), (DSA_HEADS * DSA_LATENT, t), BF16),
        ("iqt", col(IDX_HEADS * IDX_DIM), (IDX_HEADS * IDX_DIM, t), BF16),
        ("wit", col(16), (16, t), F32),
        ("ckv", row(DSA_LATENT), (t, DSA_LATENT), BF16),
        ("ckvt", col(CKVT_ROWS), (CKVT_ROWS, t), BF16),
        ("ki", row(IDX_DIM), (t, IDX_DIM), BF16),
    ]
    ins = [h1, g, w["wqk"], w["wv"], w["wo"], w["wgc"], w["wgr"], w["bgc"], w["bgr"], w["cw"], w["cb"],
           w["wdq_t"], w["wiq_t"], w["wiw_t"], w["wdc"], w["kvg"], w["wik"], w["ikg"]]
    res = pl.pallas_call(
        functools.partial(_proj_kernel, tiles_per_seq=seq // tm),
        grid=(t // tm,),
        in_specs=[row(d)] + [_resident(a.shape) for a in ins[1:]],
        out_specs=[o[1] for o in outs],
        out_shape=[jax.ShapeDtypeStruct(o[2], o[3]) for o in outs],
        scratch_shapes=[pltpu.VMEM((tm + 8, 2 * ML_WIDTH), F32)],
        compiler_params=_params("arbitrary"),
        name="proj",
    )(*ins)
    return {o[0]: r for o, r in zip(outs, res)}


def _split3(x):
    hi = x.astype(BF16)
    r1 = x - hi.astype(F32)
    mid = r1.astype(BF16)
    lo = (r1 - mid.astype(F32)).astype(BF16)
    return hi, mid, lo


def _mlstm_kernel(q_ref, k_ref, v_ref, o_ref, gc_ref, gr_ref, hg_ref, y_ref, c_ref, m_ref, *, seq):
    L = ML_CHUNK
    d = ML_HEAD_DIM
    c_ref[...] = jnp.zeros(c_ref.shape, F32)
    m_ref[...] = jnp.zeros(m_ref.shape, F32)

    ri = lax.broadcasted_iota(I32, (L, L), 0)
    ci = lax.broadcasted_iota(I32, (L, L), 1)
    causal = ci <= ri
    tri_lo = causal.astype(BF16)
    tri_up = (ri <= ci).astype(BF16)
    ones_col = (lax.broadcasted_iota(I32, (L, d), 1) == 0).astype(F32)

    nch = ML_SEQS * ML_HEADS
    causal_all = jnp.concatenate([causal] * nch, axis=0)
    stack = lambda parts: jnp.concatenate(parts, axis=0)
    col_of = lambda x4: stack([jnp.broadcast_to(x4[j:j + 1, :], (L, 1)) for j in range(nch)])
    gain = stack([jnp.broadcast_to(hg_ref[:, (j % ML_HEADS) * d:(j % ML_HEADS + 1) * d], (L, d)) for j in range(nch)])

    def chunk(c, carry):
        r0s = [pl.multiple_of(sq * seq + c * L, L) for sq in range(ML_SEQS)]
        hsl = lambda j: slice((j % ML_HEADS) * d, (j % ML_HEADS + 1) * d)
        rws = lambda j: pl.ds(r0s[j // ML_HEADS], L)
        bcs, brs, gcs, grs = [], [], [], []
        for sq in range(ML_SEQS):
            gc = gc_ref[pl.ds(r0s[sq], L), :]
            gr = gr_ref[:, pl.ds(r0s[sq], L)]
            bcs.append(sum(_dot(tri_lo, p) for p in _split3(gc)))
            brs.append(sum(_dot(p, tri_up) for p in _split3(gr)))
            gcs.append(gc)
            grs.append(gr)
        b_col = stack([bcs[j // ML_HEADS][:, ML_HEADS + j % ML_HEADS:ML_HEADS + j % ML_HEADS + 1] for j in range(nch)])
        i_col = stack([gcs[j // ML_HEADS][:, j % ML_HEADS:j % ML_HEADS + 1] for j in range(nch)])
        b_row4 = stack([brs[sq][ML_HEADS:2 * ML_HEADS, :] for sq in range(ML_SEQS)])
        i_row4 = stack([grs[sq][0:ML_HEADS, :] for sq in range(ML_SEQS)])
        u_row4 = i_row4 - b_row4
        u_blk = stack([jnp.broadcast_to(u_row4[j:j + 1, :], (L, L)) for j in range(nch)])
        m_prev4 = m_ref[0:nch, 0:1]
        logw = jnp.where(causal_all, b_col + u_blk, NEG_INF)
        inter = b_col + col_of(m_prev4)
        mj = jnp.maximum(inter, jnp.max(logw, axis=1, keepdims=True))
        w = jnp.exp(logw - mj)
        a = jnp.exp(inter - mj)
        qs = [q_ref[rws(j), hsl(j)] for j in range(nch)]
        ks = [k_ref[rws(j), hsl(j)] for j in range(nch)]
        vaugs = [jnp.concatenate([v_ref[rws(j), hsl(j)].astype(F32), ones_col], axis=1) for j in range(nch)]
        sqk = (stack([_dot_nt(qs[j], ks[j]) for j in range(nch)]) * w).astype(BF16)
        cts = [c_ref[j] for j in range(nch)]
        tot = a * stack([_dot(qs[j], cts[j].astype(BF16)) for j in range(nch)]) + stack(
            [_dot(sqk[j * L:(j + 1) * L, :], vaugs[j].astype(BF16)) for j in range(nch)])
        hh = tot[:, :d] / jnp.maximum(jnp.abs(tot[:, d:d + 1]), jnp.exp(-mj))
        gate = _sigmoid(stack([o_ref[rws(j), hsl(j)] for j in range(nch)]).astype(F32))
        y = (_rmsnorm(hh, gain) * gate).astype(BF16)
        for j in range(nch):
            y_ref[rws(j), hsl(j)] = y[j * L:(j + 1) * L, :]
        b_last4 = b_row4[:, L - 1:L]
        g_row4 = b_last4 + u_row4
        m_new4 = jnp.maximum(b_last4 + m_prev4, jnp.max(g_row4, axis=1, keepdims=True))
        dec4 = jnp.exp(b_last4 + m_prev4 - m_new4)
        wg = jnp.exp(col_of(b_last4 - m_new4) - b_col + i_col)
        for j in range(nch):
            kt = ks[j].astype(F32).T.astype(BF16)
            wv = (wg[j * L:(j + 1) * L, :] * vaugs[j]).astype(BF16)
            c_ref[j] = dec4[j:j + 1, :] * cts[j] + _dot(kt, wv)
        m_ref[0:nch, :] = jnp.broadcast_to(m_new4, (nch, m_ref.shape[1]))
        return carry

    lax.fori_loop(0, seq // L, chunk, 0)


def _mlstm(p, head_g, batch, seq):
    t = batch * seq
    rows = ML_SEQS * seq
    row = lambda n: pl.BlockSpec((rows, n), lambda b: (b, 0))
    return pl.pallas_call(
        functools.partial(_mlstm_kernel, seq=seq),
        grid=(batch // ML_SEQS,),
        in_specs=[row(ML_WIDTH), row(ML_WIDTH), row(ML_WIDTH), row(ML_WIDTH), row(128),
                  pl.BlockSpec((16, rows), lambda b: (0, b)), _full((1, ML_WIDTH))],
        out_specs=row(ML_WIDTH),
        out_shape=jax.ShapeDtypeStruct((t, ML_WIDTH), BF16),
        scratch_shapes=[pltpu.VMEM((ML_SEQS * ML_HEADS, ML_HEAD_DIM, 2 * ML_HEAD_DIM), F32),
                        pltpu.VMEM((ML_SEQS * ML_HEADS, 128), F32)],
        compiler_params=_params("parallel"),
        name="mlstm",
    )(p["q"], p["k"], p["v"], p["o"], p["gc"], p["gr"], head_g)


def _float_of_code(code):
    key = code ^ jnp.int32(INT_MIN)
    bits = jnp.where(key < 0, key ^ jnp.int32(0x7FFFFFFF), key)
    return lax.bitcast_convert_type(bits, F32)


def _colmax8(x):
    parts = [x[8 * j:8 * j + 8, :] for j in range(x.shape[0] // 8)]
    while len(parts) > 1:
        parts = [jnp.maximum(parts[j], parts[j + 1]) for j in range(0, len(parts) - 1, 2)] + (
            [parts[-1]] if len(parts) % 2 else [])
    return parts[0]


def _colsum(x):
    groups = x.shape[0] // 8
    lanes = min(2, groups)
    parts = [x[8 * j:8 * j + 8, :] for j in range(lanes)]
    for j in range(lanes, groups):
        parts[j % lanes] = parts[j % lanes] + x[8 * j:8 * j + 8, :]
    while len(parts) > 1:
        parts = [parts[j] + parts[j + 1] for j in range(0, len(parts), 2)]
    return jnp.sum(parts[0], axis=0, keepdims=True)


def _dsa_kernel(iqt_ref, wit_ref, dqt_ref, ki_ref, ckv_ref, ckvt_ref, wuvt_ref, y_ref,
                iqa_ref, dqa_ref, score_ref, bias_ref, lg_ref, acc_ref, code_ref, yt_ref, m8_ref, *, topk, nq):
    tq = iqt_ref.shape[1]
    nh = DSA_HEADS
    nk = pl.program_id(1) + 1
    last = nk - 1
    rows = lambda c: pl.ds(pl.multiple_of(c * tq, tq), tq)
    hs = lambda h: slice(h * tq, (h + 1) * tq)
    dmat = lax.broadcasted_iota(I32, (tq, tq), 0) - lax.broadcasted_iota(I32, (tq, tq), 1)
    allowed = lambda c: dmat <= jnp.where(c < last, tq, 0)

    for h in range(nh):
        iqa_ref[:, hs(h)] = iqt_ref[h * IDX_DIM:(h + 1) * IDX_DIM, :]
        dqa_ref[:, hs(h)] = dqt_ref[h * DSA_LATENT:(h + 1) * DSA_LATENT, :]
    wi_all = jnp.concatenate([wit_ref[h:h + 1, :] for h in range(IDX_HEADS)], axis=1)

    def for_blocks(body):
        def quad(p, carry):
            for u in range(4):
                body(4 * p + u)
            return carry

        def pair():
            body(nk & ~3)
            body((nk & ~3) + 1)

        lax.fori_loop(0, lax.shift_right_logical(nk, 2), quad, 0)
        pl.when((nk & 2) != 0)(pair)
        pl.when((nk & 1) != 0)(lambda: body(last))

    def idx_block(c):
        w = wi_all * jnp.maximum(_dot(ki_ref[rows(c), :], iqa_ref[...]), 0.0)
        acc = w[:, hs(0)]
        for h in range(1, IDX_HEADS):
            acc = acc + w[:, hs(h)]
        score_ref[rows(c), :] = jnp.where(allowed(c), acc, NEG_INF)

    for_blocks(idx_block)

    def search(n):
        if n * tq <= topk:
            code_ref[0:1, :] = jnp.full((1, tq), CODE_NEG_INF, I32)
            return

        def bit_pass(i, code):
            cand_code = code | lax.shift_left(jnp.int32(1), 31 - i)
            cand = _float_of_code(cand_code)
            cnt = _colsum(jnp.where(score_ref[0:n * tq, :] < cand, 0.0, 1.0))
            return jnp.where(cnt >= topk, cand_code, code)

        code_ref[0:1, :] = lax.fori_loop(0, 32, bit_pass, jnp.zeros((1, tq), I32))

    for n in range(1, nq + 1):
        pl.when(nk == n)(functools.partial(search, n))
    thr = _float_of_code(code_ref[0:1, :])

    def mask_block(c, carry):
        n_gt, n_eq = carry
        s = score_ref[rows(c), :]
        bias_ref[rows(c), :] = jnp.where((s >= thr) & allowed(c), 0.0, NEG_INF)
        n_gt = n_gt + _colsum(jnp.where(s > thr, 1.0, 0.0))
        n_eq = n_eq + _colsum(jnp.where(s == thr, 1.0, 0.0))
        return n_gt, n_eq

    zero = jnp.zeros((1, tq), F32)
    n_gt, n_eq = lax.fori_loop(0, nk, mask_block, (zero, zero))
    room = topk - n_gt

    @pl.when(jnp.max(n_eq - room) > 0.0)
    def _():
        lo = (dmat > 0).astype(BF16)

        def tie_block(c, before):
            s = score_ref[rows(c), :]
            ef = jnp.where(s == thr, 1.0, 0.0)
            rank = before + _dot(lo, ef.astype(BF16))
            keep = ((s > thr) | ((s == thr) & (rank < room))) & allowed(c)
            bias_ref[rows(c), :] = jnp.where(keep, 0.0, NEG_INF)
            return before + jnp.sum(ef, axis=0, keepdims=True)

        lax.fori_loop(0, nk, tie_block, zero)

    m8_ref[...] = jnp.full(m8_ref.shape, NEG_INF, F32)

    def logit_block(c):
        s_all = _dot(ckv_ref[rows(c), :], dqa_ref[...])
        b = bias_ref[rows(c), :]
        for h in range(nh):
            s = s_all[:, hs(h)] + b
            lg_ref[rows(c), hs(h)] = s
            m8_ref[:, hs(h)] = jnp.maximum(m8_ref[:, hs(h)], _colmax8(s))

    for_blocks(logit_block)
    m = jnp.max(m8_ref[...], axis=0, keepdims=True)
    acc_ref[...] = jnp.zeros(acc_ref.shape, F32)

    def prob_block(c):
        ckvt = ckvt_ref[:, rows(c)]
        for h in range(nh):
            p = jnp.exp2(lg_ref[rows(c), hs(h)] - m[:, hs(h)]).astype(BF16)
            acc_ref[h] += _dot(ckvt, p)

    for_blocks(prob_block)
    for h in range(nh):
        ot = acc_ref[h, 0:DSA_LATENT, :] / acc_ref[h, DSA_LATENT:DSA_LATENT + 1, :]
        yt_ref[h * DSA_HEAD_DIM:(h + 1) * DSA_HEAD_DIM, :] = _dot(wuvt_ref[h], ot.astype(BF16))
    y_ref[...] = yt_ref[...].T.astype(BF16)


def _dsa(p, wuv_t, batch, seq):
    t = batch * seq
    tq = DSA_TQ
    nq = seq // tq
    topk = min(INDEX_TOPK, seq // 4)
    qcol = lambda n: pl.BlockSpec((n, tq), lambda b, i: (0, b * nq + i))
    return pl.pallas_call(
        functools.partial(_dsa_kernel, topk=float(topk), nq=nq),
        grid=(batch, nq),
        in_specs=[qcol(IDX_HEADS * IDX_DIM), qcol(16), qcol(DSA_HEADS * DSA_LATENT),
                  pl.BlockSpec((seq, IDX_DIM), lambda b, i: (b, 0)),
                  pl.BlockSpec((seq, DSA_LATENT), lambda b, i: (b, 0)),
                  pl.BlockSpec((CKVT_ROWS, seq), lambda b, i: (0, b)),
                  _full(wuv_t.shape)],
        out_specs=pl.BlockSpec((tq, DSA_WIDTH), lambda b, i: (b * nq + i, 0)),
        out_shape=jax.ShapeDtypeStruct((t, DSA_WIDTH), BF16),
        scratch_shapes=[pltpu.VMEM((IDX_DIM, IDX_HEADS * tq), BF16), pltpu.VMEM((DSA_LATENT, DSA_HEADS * tq), BF16),
                        pltpu.VMEM((seq, tq), F32), pltpu.VMEM((seq, tq), F32),
                        pltpu.VMEM((seq, DSA_HEADS * tq), F32), pltpu.VMEM((DSA_HEADS, CKVT_ROWS, tq), F32),
                        pltpu.VMEM((8, tq), I32), pltpu.VMEM((DSA_WIDTH, tq), F32),
                        pltpu.VMEM((8, DSA_HEADS * tq), F32)],
        compiler_params=_params("parallel", "arbitrary"),
        name="dsa",
    )(p["iqt"], p["wit"], p["dqt"], p["ki"], p["ckv"], p["ckvt"], wuv_t)


def _memkv_kernel(m_ref, g_ref, w_ref, o_ref):
    o_ref[...] = _dot(_rmsnorm(m_ref[...], g_ref[...]).astype(BF16), w_ref[...].astype(BF16)).astype(BF16)


def _memkv(mem2d, g, w_kv, mem_len):
    rows, d = mem2d.shape
    return pl.pallas_call(
        _memkv_kernel,
        grid=(rows // mem_len,),
        in_specs=[pl.BlockSpec((mem_len, d), lambda b: (b, 0)), _full((1, d)), _resident(w_kv.shape)],
        out_specs=pl.BlockSpec((mem_len, w_kv.shape[1]), lambda b: (b, 0)),
        out_shape=jax.ShapeDtypeStruct((rows, w_kv.shape[1]), BF16),
        compiler_params=_params("parallel"),
        name="memkv",
    )(mem2d, g, w_kv)


def _mixout_kernel(h_ref, yml_ref, ydsa_ref, woa_ref, wob_ref, g_ref, wq_ref, kv_ref, wo_ref, o_ref):
    d = h_ref.shape[1]
    hd = d // XA_HEADS
    h2 = h_ref[...] + _dot(yml_ref[...], woa_ref[...].astype(BF16)) + _dot(ydsa_ref[...], wob_ref[...].astype(BF16))
    q = _dot(_rmsnorm(h2, g_ref[...]).astype(BF16), wq_ref[...].astype(BF16)).astype(BF16)
    heads = []
    for h in range(XA_HEADS):
        kh = kv_ref[:, h * hd:(h + 1) * hd]
        vh = kv_ref[:, d + h * hd:d + (h + 1) * hd]
        lg = _dot_nt(q[:, h * hd:(h + 1) * hd], kh) * (hd ** -0.5)
        p = jnp.exp(lg - jnp.max(lg, axis=1, keepdims=True))
        denom = jnp.sum(p, axis=1, keepdims=True)
        heads.append((_dot(p.astype(BF16), vh) / denom).astype(BF16))
    o_ref[...] = h2 + _dot(jnp.concatenate(heads, axis=1), wo_ref[...].astype(BF16))


def _mixout(h1, yml, ydsa, w, kv, batch, seq, mem_len):
    t, d = h1.shape
    tm = MIX_TM
    nt = seq // tm
    row = lambda n: pl.BlockSpec((tm, n), lambda b, j: (b * nt + j, 0))
    return pl.pallas_call(
        _mixout_kernel,
        grid=(batch, nt),
        in_specs=[row(d), row(ML_WIDTH), row(DSA_WIDTH), _resident(w["woa"].shape), _resident(w["wob"].shape),
                  _full((1, d)), _resident(w["wq"].shape),
                  pl.BlockSpec((mem_len, 2 * d), lambda b, j: (b, 0)), _resident(w["wxo"].shape)],
        out_specs=row(d),
        out_shape=jax.ShapeDtypeStruct((t, d), F32),
        compiler_params=_params("parallel", "parallel"),
        name="mixout",
    )(h1, yml, ydsa, w["woa"], w["wob"], w["xg"], w["wq"], kv, w["wxo"])


def _layer(h, mem2d, p, batch, seq, mem_len, final_g):
    h1 = _ffn(h, p["f1g"], p["f1wg"], p["f1wu"], p["f1wd"], p["f1g"], final_norm=False)
    pr = _proj(h1, p["mixg"], p, seq)
    yml = _mlstm(pr, p["headg"], batch, seq)
    ydsa = _dsa(pr, p["wuv_t"], batch, seq)
    kv = _memkv(mem2d, p["memg"], p["wkv"], mem_len)
    h3 = _mixout(h1, yml, ydsa, p, kv, batch, seq, mem_len)
    fg = p["f2g"] if final_g is None else final_g
    return _ffn(h3, p["f2g"], p["f2wg"], p["f2wu"], p["f2wd"], fg, final_norm=final_g is not None)


def kernel(x, mem, ffn1_norm_g, ffn1_w_gate, ffn1_w_up, ffn1_w_down, mix_norm_g, w_in, mlstm_conv_w, mlstm_conv_b, mlstm_i_bias, mlstm_f_bias, mlstm_head_norm_g, dsa_kv_norm_g, idx_k_norm_g, dsa_w_uv, w_out, xattn_norm_g, mem_norm_g, xattn_w_q, xattn_w_kv, xattn_w_o, ffn2_norm_g, ffn2_w_gate, ffn2_w_up, ffn2_w_down, final_norm_g):
    batch, seq, d = x.shape
    mem_len = mem.shape[1]
    depth = w_in.shape[0]
    h = x.reshape(batch * seq, d)
    mem2d = mem.reshape(batch * mem_len, d)
    row = lambda a: a.reshape(1, -1).astype(F32)
    b16 = lambda a: a.astype(BF16)

    splits = (ML_WIDTH, ML_WIDTH, ML_WIDTH, ML_HEADS, ML_HEADS, ML_WIDTH, DSA_HEADS * DSA_LATENT, DSA_LATENT,
              IDX_HEADS * IDX_DIM, IDX_DIM, IDX_HEADS)
    offs = [0]
    for s in splits:
        offs.append(offs[-1] + s)

    for l in range(depth):
        wi = w_in[l]
        cols = [wi[:, offs[i]:offs[i + 1]] for i in range(len(splits))]
        mq, mk, mv, mi, mf, mo, dq, dc, iq, ik, iw = cols
        w_gate = jnp.concatenate([mi, mf], axis=1)
        b_gate = jnp.concatenate([mlstm_i_bias[l], mlstm_f_bias[l]])
        p = {
            "f1g": row(ffn1_norm_g[l]), "f1wg": ffn1_w_gate[l], "f1wu": ffn1_w_up[l], "f1wd": ffn1_w_down[l],
            "f2g": row(ffn2_norm_g[l]), "f2wg": ffn2_w_gate[l], "f2wu": ffn2_w_up[l], "f2wd": ffn2_w_down[l],
            "mixg": row(mix_norm_g[l]),
            "wqk": b16(jnp.concatenate([mq, mk], axis=1)), "wv": b16(mv), "wo": b16(mo),
            "wgc": b16(jnp.pad(w_gate, ((0, 0), (0, 128 - 2 * ML_HEADS)))),
            "wgr": b16(jnp.pad(w_gate.T, ((0, 16 - 2 * ML_HEADS), (0, 0)))),
            "bgc": jnp.pad(b_gate, (0, 128 - 2 * ML_HEADS)).reshape(1, 128).astype(F32),
            "bgr": jnp.pad(b_gate, (0, 16 - 2 * ML_HEADS)).reshape(16, 1).astype(F32),
            "cw": mlstm_conv_w[l].astype(F32), "cb": row(mlstm_conv_b[l]),
            "wdq_t": b16(dq.T), "wiq_t": b16(iq.T), "wiw_t": b16(jnp.pad(iw.T, ((0, 16 - IDX_HEADS), (0, 0)))),
            "wdc": b16(dc), "kvg": row(dsa_kv_norm_g[l]),
            "wik": b16(jnp.pad(ik, ((0, 0), (0, 128 - IDX_DIM)))), "ikg": row(idx_k_norm_g[l]),
            "headg": row(mlstm_head_norm_g[l]),
            "wuv_t": b16(jnp.swapaxes(dsa_w_uv[l], 1, 2)),
            "woa": w_out[l][:ML_WIDTH], "wob": w_out[l][ML_WIDTH:],
            "xg": row(xattn_norm_g[l]), "memg": row(mem_norm_g[l]),
            "wq": xattn_w_q[l], "wkv": xattn_w_kv[l], "wxo": xattn_w_o[l],
        }
        fg = row(final_norm_g) if l == depth - 1 else None
        h = _layer(h, mem2d, p, batch, seq, mem_len, fg)
    return h.reshape(batch, seq, d)
```

```python
import functools

import jax
import jax.numpy as jnp
from jax import lax
from jax.experimental import pallas as pl
from jax.experimental.pallas import tpu as pltpu

F32 = jnp.float32
BF16 = jnp.bfloat16
I32 = jnp.int32
EPS = 1e-6

ML_HEADS = 4
ML_HEAD_DIM = 128
ML_WIDTH = ML_HEADS * ML_HEAD_DIM
ML_CONV = 4
DSA_HEADS = 8
DSA_LATENT = 128
DSA_HEAD_DIM = 64
DSA_WIDTH = DSA_HEADS * DSA_HEAD_DIM
IDX_HEADS = 8
IDX_DIM = 64
INDEX_TOPK = 256
XA_HEADS = 4

FFN_TM = 512
FFN_FC = 256
PROJ_TM = 1024
ML_CHUNK = 128
ML_SEQS = 2
DSA_TQ = 256
MIX_TM = 512

CKVT_ROWS = DSA_LATENT + 16

VMEM_LIMIT = 56 * 1024 * 1024
INT_MIN = -(2 ** 31)
NEG_INF = float("-inf")
CODE_NEG_INF = 0x007FFFFF
LOG2E = 1.4426950408889634


def _rmsnorm(x, g):
    return x * lax.rsqrt(jnp.mean(x * x, axis=-1, keepdims=True) + EPS) * g


def _sigmoid(x):
    return 1.0 / (1.0 + jnp.exp(-x))


def _dot(a, b):
    return jnp.dot(a, b, preferred_element_type=F32)


def _dot_nt(a, b):
    return lax.dot_general(a, b, (((1,), (1,)), ((), ())), preferred_element_type=F32)


def _full(shape):
    return pl.BlockSpec(shape, lambda *_: (0,) * len(shape))


def _resident(shape):
    return pl.BlockSpec(shape, lambda *_: (0,) * len(shape), pipeline_mode=pl.Buffered(1))


def _params(*sem):
    return pltpu.CompilerParams(dimension_semantics=sem, vmem_limit_bytes=VMEM_LIMIT)


def _ffn_kernel(x_ref, g_ref, wg_ref, wu_ref, wd_ref, fg_ref, o_ref, *, final_norm):
    x = x_ref[...]
    xn = _rmsnorm(x, g_ref[...]).astype(BF16)
    d_ff = wg_ref.shape[1]
    acc = jnp.zeros(x.shape, F32)
    for c in range(d_ff // FFN_FC):
        sl = slice(c * FFN_FC, (c + 1) * FFN_FC)
        gate = _dot(xn, wg_ref[:, sl].astype(BF16))
        up = _dot(xn, wu_ref[:, sl].astype(BF16))
        act = (gate * _sigmoid(gate) * up).astype(BF16)
        acc = acc + _dot(act, wd_ref[sl, :].astype(BF16))
    y = x + 0.5 * acc
    if final_norm:
        y = _rmsnorm(y, fg_ref[...])
    o_ref[...] = y


def _ffn(x, g, wg, wu, wd, fg, final_norm):
    t, d = x.shape
    d_ff = wg.shape[1]
    return pl.pallas_call(
        functools.partial(_ffn_kernel, final_norm=final_norm),
        grid=(t // FFN_TM,),
        in_specs=[
            pl.BlockSpec((FFN_TM, d), lambda i: (i, 0)),
            _full((1, d)),
            _resident((d, d_ff)),
            _resident((d, d_ff)),
            _resident((d_ff, d)),
            _full((1, d)),
        ],
        out_specs=pl.BlockSpec((FFN_TM, d), lambda i: (i, 0)),
        out_shape=jax.ShapeDtypeStruct((t, d), F32),
        compiler_params=_params("parallel"),
        name="ffn_final" if final_norm else "ffn",
    )(x, g, wg, wu, wd, fg)


def _proj_kernel(h_ref, g_ref, wqk_ref, wv_ref, wo_ref, wgc_ref, wgr_ref, bgc_ref, bgr_ref,
                 cw_ref, cb_ref, wdq_ref, wiq_ref, wiw_ref, wdc_ref, kvg_ref, wik_ref, ikg_ref,
                 q_ref, k_ref, v_ref, o_ref, gc_ref, gr_ref, dqt_ref, iqt_ref, wit_ref,
                 ckv_ref, ckvt_ref, ki_ref, zbuf, *, tiles_per_seq):
    tm = h_ref.shape[0]
    u = _rmsnorm(h_ref[...], g_ref[...]).astype(BF16)

    @pl.when(pl.program_id(0) % tiles_per_seq == 0)
    def _():
        zbuf[0:8, :] = jnp.zeros((8, zbuf.shape[1]), F32)

    zbuf[8:8 + tm, :] = _dot(u, wqk_ref[...])
    y = jnp.zeros((tm, zbuf.shape[1]), F32) + cb_ref[...]
    for j in range(ML_CONV):
        y = y + zbuf[5 + j:5 + j + tm, :] * cw_ref[j:j + 1, :]
    zbuf[0:8, :] = zbuf[tm:tm + 8, :]
    y = y * _sigmoid(y)
    q_ref[...] = y[:, :ML_WIDTH].astype(BF16)
    k_ref[...] = (y[:, ML_WIDTH:] * (ML_HEAD_DIM ** -0.5)).astype(BF16)
    v_ref[...] = _dot(u, wv_ref[...]).astype(BF16)
    o_ref[...] = _dot(u, wo_ref[...]).astype(BF16)

    def gates(z, idx):
        return jnp.where(idx < ML_HEADS, z, jnp.minimum(z, 0.0) - jnp.log(1.0 + jnp.exp(-jnp.abs(z))))

    zc = _dot(u, wgc_ref[...]) + bgc_ref[...]
    gc_ref[...] = gates(zc, lax.broadcasted_iota(I32, zc.shape, 1))
    zr = _dot_nt(wgr_ref[...], u) + bgr_ref[...]
    gr_ref[...] = gates(zr, lax.broadcasted_iota(I32, zr.shape, 0))

    dqt_ref[...] = (_dot_nt(wdq_ref[...], u) * (DSA_LATENT ** -0.5 * LOG2E)).astype(BF16)
    iqt_ref[...] = _dot_nt(wiq_ref[...], u).astype(BF16)
    wit_ref[...] = _dot_nt(wiw_ref[...], u) * (IDX_HEADS ** -0.5) * (IDX_DIM ** -0.5)
    ckv = _rmsnorm(_dot(u, wdc_ref[...]), kvg_ref[...])
    ckv_ref[...] = ckv.astype(BF16)
    ckvt_ref[0:DSA_LATENT, :] = ckv.T.astype(BF16)
    ckvt_ref[DSA_LATENT:, :] = (lax.broadcasted_iota(I32, (CKVT_ROWS - DSA_LATENT, tm), 0) == 0).astype(BF16)
    ik = _dot(u, wik_ref[...])[:, :IDX_DIM]
    ki_ref[...] = _rmsnorm(ik, ikg_ref[...]).astype(BF16)


def _proj(h1, g, w, seq):
    t, d = h1.shape
    tm = PROJ_TM
    row = lambda n: pl.BlockSpec((tm, n), lambda i: (i, 0))
    col = lambda n: pl.BlockSpec((n, tm), lambda i: (0, i))
    outs = [
        ("q", row(ML_WIDTH), (t, ML_WIDTH), BF16),
        ("k", row(ML_WIDTH), (t, ML_WIDTH), BF16),
        ("v", row(ML_WIDTH), (t, ML_WIDTH), BF16),
        ("o", row(ML_WIDTH), (t, ML_WIDTH), BF16),
        ("gc", row(128), (t, 128), F32),
        ("gr", col(16), (16, t), F32),
        ("dqt", col(DSA_HEADS * DSA_LATENT), (DSA_HEADS * DSA_LATENT, t), BF16),
        ("iqt", col(IDX_HEADS * IDX_DIM), (IDX_HEADS * IDX_DIM, t), BF16),
        ("wit", col(16), (16, t), F32),
        ("ckv", row(DSA_LATENT), (t, DSA_LATENT), BF16),
        ("ckvt", col(CKVT_ROWS), (CKVT_ROWS, t), BF16),
        ("ki", row(IDX_DIM), (t, IDX_DIM), BF16),
    ]
    ins = [h1, g, w["wqk"], w["wv"], w["wo"], w["wgc"], w["wgr"], w["bgc"], w["bgr"], w["cw"], w["cb"],
           w["wdq_t"], w["wiq_t"], w["wiw_t"], w["wdc"], w["kvg"], w["wik"], w["ikg"]]
    res = pl.pallas_call(
        functools.partial(_proj_kernel, tiles_per_seq=seq // tm),
        grid=(t // tm,),
        in_specs=[row(d)] + [_resident(a.shape) for a in ins[1:]],
        out_specs=[o[1] for o in outs],
        out_shape=[jax.ShapeDtypeStruct(o[2], o[3]) for o in outs],
        scratch_shapes=[pltpu.VMEM((tm + 8, 2 * ML_WIDTH), F32)],
        compiler_params=_params("arbitrary"),
        name="proj",
    )(*ins)
    return {o[0]: r for o, r in zip(outs, res)}


def _split3(x):
    hi = x.astype(BF16)
    r1 = x - hi.astype(F32)
    mid = r1.astype(BF16)
    lo = (r1 - mid.astype(F32)).astype(BF16)
    return hi, mid, lo


def _mlstm_kernel(q_ref, k_ref, v_ref, o_ref, gc_ref, gr_ref, hg_ref, y_ref, c_ref, m_ref, *, seq):
    L = ML_CHUNK
    d = ML_HEAD_DIM
    c_ref[...] = jnp.zeros(c_ref.shape, F32)
    m_ref[...] = jnp.zeros(m_ref.shape, F32)

    ri = lax.broadcasted_iota(I32, (L, L), 0)
    ci = lax.broadcasted_iota(I32, (L, L), 1)
    causal = ci <= ri
    tri_lo = causal.astype(BF16)
    tri_up = (ri <= ci).astype(BF16)
    ones_col = (lax.broadcasted_iota(I32, (L, d), 1) == 0).astype(F32)

    nch = ML_SEQS * ML_HEADS
    causal_all = jnp.concatenate([causal] * nch, axis=0)
    stack = lambda parts: jnp.concatenate(parts, axis=0)
    col_of = lambda x4: stack([jnp.broadcast_to(x4[j:j + 1, :], (L, 1)) for j in range(nch)])
    gain = stack([jnp.broadcast_to(hg_ref[:, (j % ML_HEADS) * d:(j % ML_HEADS + 1) * d], (L, d)) for j in range(nch)])

    def chunk(c, carry):
        r0s = [pl.multiple_of(sq * seq + c * L, L) for sq in range(ML_SEQS)]
        hsl = lambda j: slice((j % ML_HEADS) * d, (j % ML_HEADS + 1) * d)
        rws = lambda j: pl.ds(r0s[j // ML_HEADS], L)
        bcs, brs, gcs, grs = [], [], [], []
        for sq in range(ML_SEQS):
            gc = gc_ref[pl.ds(r0s[sq], L), :]
            gr = gr_ref[:, pl.ds(r0s[sq], L)]
            bcs.append(sum(_dot(tri_lo, p) for p in _split3(gc)))
            brs.append(sum(_dot(p, tri_up) for p in _split3(gr)))
            gcs.append(gc)
            grs.append(gr)
        b_col = stack([bcs[j // ML_HEADS][:, ML_HEADS + j % ML_HEADS:ML_HEADS + j % ML_HEADS + 1] for j in range(nch)])
        i_col = stack([gcs[j // ML_HEADS][:, j % ML_HEADS:j % ML_HEADS + 1] for j in range(nch)])
        b_row4 = stack([brs[sq][ML_HEADS:2 * ML_HEADS, :] for sq in range(ML_SEQS)])
        i_row4 = stack([grs[sq][0:ML_HEADS, :] for sq in range(ML_SEQS)])
        u_row4 = i_row4 - b_row4
        u_blk = stack([jnp.broadcast_to(u_row4[j:j + 1, :], (L, L)) for j in range(nch)])
        m_prev4 = m_ref[0:nch, 0:1]
        logw = jnp.where(causal_all, b_col + u_blk, NEG_INF)
        inter = b_col + col_of(m_prev4)
        mj = jnp.maximum(inter, jnp.max(logw, axis=1, keepdims=True))
        w = jnp.exp(logw - mj)
        a = jnp.exp(inter - mj)
        qs = [q_ref[rws(j), hsl(j)] for j in range(nch)]
        ks = [k_ref[rws(j), hsl(j)] for j in range(nch)]
        vaugs = [jnp.concatenate([v_ref[rws(j), hsl(j)].astype(F32), ones_col], axis=1) for j in range(nch)]
        sqk = (stack([_dot_nt(qs[j], ks[j]) for j in range(nch)]) * w).astype(BF16)
        cts = [c_ref[j] for j in range(nch)]
        tot = a * stack([_dot(qs[j], cts[j].astype(BF16)) for j in range(nch)]) + stack(
            [_dot(sqk[j * L:(j + 1) * L, :], vaugs[j].astype(BF16)) for j in range(nch)])
        hh = tot[:, :d] / jnp.maximum(jnp.abs(tot[:, d:d + 1]), jnp.exp(-mj))
        gate = _sigmoid(stack([o_ref[rws(j), hsl(j)] for j in range(nch)]).astype(F32))
        y = (_rmsnorm(hh, gain) * gate).astype(BF16)
        for j in range(nch):
            y_ref[rws(j), hsl(j)] = y[j * L:(j + 1) * L, :]
        b_last4 = b_row4[:, L - 1:L]
        g_row4 = b_last4 + u_row4
        m_new4 = jnp.maximum(b_last4 + m_prev4, jnp.max(g_row4, axis=1, keepdims=True))
        dec4 = jnp.exp(b_last4 + m_prev4 - m_new4)
        wg = jnp.exp(col_of(b_last4 - m_new4) - b_col + i_col)
        for j in range(nch):
            kt = ks[j].astype(F32).T.astype(BF16)
            wv = (wg[j * L:(j + 1) * L, :] * vaugs[j]).astype(BF16)
            c_ref[j] = dec4[j:j + 1, :] * cts[j] + _dot(kt, wv)
        m_ref[0:nch, :] = jnp.broadcast_to(m_new4, (nch, m_ref.shape[1]))
        return carry

    lax.fori_loop(0, seq // L, chunk, 0)


def _mlstm(p, head_g, batch, seq):
    t = batch * seq
    rows = ML_SEQS * seq
    row = lambda n: pl.BlockSpec((rows, n), lambda b: (b, 0))
    return pl.pallas_call(
        functools.partial(_mlstm_kernel, seq=seq),
        grid=(batch // ML_SEQS,),
        in_specs=[row(ML_WIDTH), row(ML_WIDTH), row(ML_WIDTH), row(ML_WIDTH), row(128),
                  pl.BlockSpec((16, rows), lambda b: (0, b)), _full((1, ML_WIDTH))],
        out_specs=row(ML_WIDTH),
        out_shape=jax.ShapeDtypeStruct((t, ML_WIDTH), BF16),
        scratch_shapes=[pltpu.VMEM((ML_SEQS * ML_HEADS, ML_HEAD_DIM, 2 * ML_HEAD_DIM), F32),
                        pltpu.VMEM((ML_SEQS * ML_HEADS, 128), F32)],
        compiler_params=_params("parallel"),
        name="mlstm",
    )(p["q"], p["k"], p["v"], p["o"], p["gc"], p["gr"], head_g)


def _float_of_code(code):
    key = code ^ jnp.int32(INT_MIN)
    bits = jnp.where(key < 0, key ^ jnp.int32(0x7FFFFFFF), key)
    return lax.bitcast_convert_type(bits, F32)


def _colmax8(x):
    parts = [x[8 * j:8 * j + 8, :] for j in range(x.shape[0] // 8)]
    while len(parts) > 1:
        parts = [jnp.maximum(parts[j], parts[j + 1]) for j in range(0, len(parts) - 1, 2)] + (
            [parts[-1]] if len(parts) % 2 else [])
    return parts[0]


def _colsum(x):
    groups = x.shape[0] // 8
    lanes = min(2, groups)
    parts = [x[8 * j:8 * j + 8, :] for j in range(lanes)]
    for j in range(lanes, groups):
        parts[j % lanes] = parts[j % lanes] + x[8 * j:8 * j + 8, :]
    while len(parts) > 1:
        parts = [parts[j] + parts[j + 1] for j in range(0, len(parts), 2)]
    return jnp.sum(parts[0], axis=0, keepdims=True)


def _colsum_bf16(x):
    groups = x.shape[0] // 16
    assert x.shape[0] % 16 == 0 and groups <= 2 * 256
    lanes = min(2, groups)
    parts = [x[16 * j:16 * j + 16, :] for j in range(lanes)]
    for j in range(lanes, groups):
        parts[j % lanes] = parts[j % lanes] + x[16 * j:16 * j + 16, :]
    tot = parts[0].astype(F32)
    for part in parts[1:]:
        tot = tot + part.astype(F32)
    return jnp.sum(tot, axis=0, keepdims=True)


def _dsa_kernel(iqt_ref, wit_ref, dqt_ref, ki_ref, ckv_ref, ckvt_ref, wuvt_ref, y_ref,
                iqa_ref, dqa_ref, score_ref, sb_ref, bias_ref, lg_ref, acc_ref, code_ref, yt_ref, m8_ref, *, topk, nq):
    tq = iqt_ref.shape[1]
    nh = DSA_HEADS
    nk = pl.program_id(1) + 1
    last = nk - 1
    rows = lambda c: pl.ds(pl.multiple_of(c * tq, tq), tq)
    hs = lambda h: slice(h * tq, (h + 1) * tq)
    dmat = lax.broadcasted_iota(I32, (tq, tq), 0) - lax.broadcasted_iota(I32, (tq, tq), 1)
    allowed = lambda c: dmat <= jnp.where(c < last, tq, 0)

    for h in range(nh):
        iqa_ref[:, hs(h)] = iqt_ref[h * IDX_DIM:(h + 1) * IDX_DIM, :]
        dqa_ref[:, hs(h)] = dqt_ref[h * DSA_LATENT:(h + 1) * DSA_LATENT, :]
    wi_all = jnp.concatenate([wit_ref[h:h + 1, :] for h in range(IDX_HEADS)], axis=1)

    def for_blocks(body):
        def quad(p, carry):
            for u in range(4):
                body(4 * p + u)
            return carry

        def pair():
            body(nk & ~3)
            body((nk & ~3) + 1)

        lax.fori_loop(0, lax.shift_right_logical(nk, 2), quad, 0)
        pl.when((nk & 2) != 0)(pair)
        pl.when((nk & 1) != 0)(lambda: body(last))

    def idx_block(c):
        w = wi_all * jnp.maximum(_dot(ki_ref[rows(c), :], iqa_ref[...]), 0.0)
        acc = w[:, hs(0)]
        for h in range(1, IDX_HEADS):
            acc = acc + w[:, hs(h)]
        sc = jnp.where(allowed(c), acc, NEG_INF)
        score_ref[rows(c), :] = sc
        sb_ref[rows(c), :] = sc.astype(BF16)

    for_blocks(idx_block)

    def search(n):
        if n * tq <= topk:
            code_ref[0:1, :] = jnp.full((1, tq), CODE_NEG_INF, I32)
            return

        def hi_pass(i, code):
            cand_code = code | lax.shift_left(jnp.int32(1), 15 - i)
            cand = _float_of_code(lax.shift_left(cand_code, 16)).astype(BF16)
            ind = jnp.where(sb_ref[0:n * tq, :] < cand, jnp.zeros((), BF16), jnp.ones((), BF16))
            return jnp.where(_colsum_bf16(ind) >= topk, cand_code, code)

        c1 = lax.fori_loop(0, 16, hi_pass, jnp.zeros((1, tq), I32))
        base = lax.shift_left(c1, 16) - jnp.int32(0x8001)

        def lo_pass(i, off):
            cand_off = off | lax.shift_left(jnp.int32(1), 16 - i)
            cand = _float_of_code(base + cand_off)
            cnt = _colsum(jnp.where(score_ref[0:n * tq, :] < cand, 0.0, 1.0))
            return jnp.where(cnt >= topk, cand_off, off)

        code_ref[0:1, :] = base + lax.fori_loop(0, 17, lo_pass, jnp.zeros((1, tq), I32))

    for n in range(1, nq + 1):
        pl.when(nk == n)(functools.partial(search, n))
    thr = _float_of_code(code_ref[0:1, :])

    def mask_block(c, carry):
        n_gt, n_eq = carry
        s = score_ref[rows(c), :]
        bias_ref[rows(c), :] = jnp.where((s >= thr) & allowed(c), 0.0, NEG_INF)
        n_gt = n_gt + _colsum(jnp.where(s > thr, 1.0, 0.0))
        n_eq = n_eq + _colsum(jnp.where(s == thr, 1.0, 0.0))
        return n_gt, n_eq

    zero = jnp.zeros((1, tq), F32)
    n_gt, n_eq = lax.fori_loop(0, nk, mask_block, (zero, zero))
    room = topk - n_gt

    @pl.when(jnp.max(n_eq - room) > 0.0)
    def _():
        lo = (dmat > 0).astype(BF16)

        def tie_block(c, before):
            s = score_ref[rows(c), :]
            ef = jnp.where(s == thr, 1.0, 0.0)
            rank = before + _dot(lo, ef.astype(BF16))
            keep = ((s > thr) | ((s == thr) & (rank < room))) & allowed(c)
            bias_ref[rows(c), :] = jnp.where(keep, 0.0, NEG_INF)
            return before + jnp.sum(ef, axis=0, keepdims=True)

        lax.fori_loop(0, nk, tie_block, zero)

    m8_ref[...] = jnp.full(m8_ref.shape, NEG_INF, F32)

    def logit_block(c):
        s_all = _dot(ckv_ref[rows(c), :], dqa_ref[...])
        b = bias_ref[rows(c), :]
        for h in range(nh):
            s = s_all[:, hs(h)] + b
            lg_ref[rows(c), hs(h)] = s
            m8_ref[:, hs(h)] = jnp.maximum(m8_ref[:, hs(h)], _colmax8(s))

    for_blocks(logit_block)
    m = jnp.max(m8_ref[...], axis=0, keepdims=True)
    acc_ref[...] = jnp.zeros(acc_ref.shape, F32)

    def prob_block(c):
        ckvt = ckvt_ref[:, rows(c)]
        for h in range(nh):
            p = jnp.exp2(lg_ref[rows(c), hs(h)] - m[:, hs(h)]).astype(BF16)
            acc_ref[h] += _dot(ckvt, p)

    for_blocks(prob_block)
    for h in range(nh):
        ot = acc_ref[h, 0:DSA_LATENT, :] / acc_ref[h, DSA_LATENT:DSA_LATENT + 1, :]
        yt_ref[h * DSA_HEAD_DIM:(h + 1) * DSA_HEAD_DIM, :] = _dot(wuvt_ref[h], ot.astype(BF16))
    y_ref[...] = yt_ref[...].T.astype(BF16)


def _dsa(p, wuv_t, batch, seq):
    t = batch * seq
    tq = DSA_TQ
    nq = seq // tq
    topk = min(INDEX_TOPK, seq // 4)
    qcol = lambda n: pl.BlockSpec((n, tq), lambda b, i: (0, b * nq + i))
    return pl.pallas_call(
        functools.partial(_dsa_kernel, topk=float(topk), nq=nq),
        grid=(batch, nq),
        in_specs=[qcol(IDX_HEADS * IDX_DIM), qcol(16), qcol(DSA_HEADS * DSA_LATENT),
                  pl.BlockSpec((seq, IDX_DIM), lambda b, i: (b, 0)),
                  pl.BlockSpec((seq, DSA_LATENT), lambda b, i: (b, 0)),
                  pl.BlockSpec((CKVT_ROWS, seq), lambda b, i: (0, b)),
                  _full(wuv_t.shape)],
        out_specs=pl.BlockSpec((tq, DSA_WIDTH), lambda b, i: (b * nq + i, 0)),
        out_shape=jax.ShapeDtypeStruct((t, DSA_WIDTH), BF16),
        scratch_shapes=[pltpu.VMEM((IDX_DIM, IDX_HEADS * tq), BF16), pltpu.VMEM((DSA_LATENT, DSA_HEADS * tq), BF16),
                        pltpu.VMEM((seq, tq), F32), pltpu.VMEM((seq, tq), BF16), pltpu.VMEM((seq, tq), F32),
                        pltpu.VMEM((seq, DSA_HEADS * tq), F32), pltpu.VMEM((DSA_HEADS, CKVT_ROWS, tq), F32),
                        pltpu.VMEM((8, tq), I32), pltpu.VMEM((DSA_WIDTH, tq), F32),
                        pltpu.VMEM((8, DSA_HEADS * tq), F32)],
        compiler_params=_params("parallel", "arbitrary"),
        name="dsa",
    )(p["iqt"], p["wit"], p["dqt"], p["ki"], p["ckv"], p["ckvt"], wuv_t)


def _memkv_kernel(m_ref, g_ref, w_ref, o_ref):
    o_ref[...] = _dot(_rmsnorm(m_ref[...], g_ref[...]).astype(BF16), w_ref[...].astype(BF16)).astype(BF16)


def _memkv(mem2d, g, w_kv, mem_len):
    rows, d = mem2d.shape
    return pl.pallas_call(
        _memkv_kernel,
        grid=(rows // mem_len,),
        in_specs=[pl.BlockSpec((mem_len, d), lambda b: (b, 0)), _full((1, d)), _resident(w_kv.shape)],
        out_specs=pl.BlockSpec((mem_len, w_kv.shape[1]), lambda b: (b, 0)),
        out_shape=jax.ShapeDtypeStruct((rows, w_kv.shape[1]), BF16),
        compiler_params=_params("parallel"),
        name="memkv",
    )(mem2d, g, w_kv)


def _mixout_kernel(h_ref, yml_ref, ydsa_ref, woa_ref, wob_ref, g_ref, wq_ref, kv_ref, wo_ref, o_ref):
    d = h_ref.shape[1]
    hd = d // XA_HEADS
    h2 = h_ref[...] + _dot(yml_ref[...], woa_ref[...].astype(BF16)) + _dot(ydsa_ref[...], wob_ref[...].astype(BF16))
    q = _dot(_rmsnorm(h2, g_ref[...]).astype(BF16), wq_ref[...].astype(BF16)).astype(BF16)
    heads = []
    for h in range(XA_HEADS):
        kh = kv_ref[:, h * hd:(h + 1) * hd]
        vh = kv_ref[:, d + h * hd:d + (h + 1) * hd]
        lg = _dot_nt(q[:, h * hd:(h + 1) * hd], kh) * (hd ** -0.5)
        p = jnp.exp(lg - jnp.max(lg, axis=1, keepdims=True))
        denom = jnp.sum(p, axis=1, keepdims=True)
        heads.append((_dot(p.astype(BF16), vh) / denom).astype(BF16))
    o_ref[...] = h2 + _dot(jnp.concatenate(heads, axis=1), wo_ref[...].astype(BF16))


def _mixout(h1, yml, ydsa, w, kv, batch, seq, mem_len):
    t, d = h1.shape
    tm = MIX_TM
    nt = seq // tm
    row = lambda n: pl.BlockSpec((tm, n), lambda b, j: (b * nt + j, 0))
    return pl.pallas_call(
        _mixout_kernel,
        grid=(batch, nt),
        in_specs=[row(d), row(ML_WIDTH), row(DSA_WIDTH), _resident(w["woa"].shape), _resident(w["wob"].shape),
                  _full((1, d)), _resident(w["wq"].shape),
                  pl.BlockSpec((mem_len, 2 * d), lambda b, j: (b, 0)), _resident(w["wxo"].shape)],
        out_specs=row(d),
        out_shape=jax.ShapeDtypeStruct((t, d), F32),
        compiler_params=_params("parallel", "parallel"),
        name="mixout",
    )(h1, yml, ydsa, w["woa"], w["wob"], w["xg"], w["wq"], kv, w["wxo"])


def _layer(h, mem2d, p, batch, seq, mem_len, final_g):
    h1 = _ffn(h, p["f1g"], p["f1wg"], p["f1wu"], p["f1wd"], p["f1g"], final_norm=False)
    pr = _proj(h1, p["mixg"], p, seq)
    yml = _mlstm(pr, p["headg"], batch, seq)
    ydsa = _dsa(pr, p["wuv_t"], batch, seq)
    kv = _memkv(mem2d, p["memg"], p["wkv"], mem_len)
    h3 = _mixout(h1, yml, ydsa, p, kv, batch, seq, mem_len)
    fg = p["f2g"] if final_g is None else final_g
    return _ffn(h3, p["f2g"], p["f2wg"], p["f2wu"], p["f2wd"], fg, final_norm=final_g is not None)


def kernel(x, mem, ffn1_norm_g, ffn1_w_gate, ffn1_w_up, ffn1_w_down, mix_norm_g, w_in, mlstm_conv_w, mlstm_conv_b, mlstm_i_bias, mlstm_f_bias, mlstm_head_norm_g, dsa_kv_norm_g, idx_k_norm_g, dsa_w_uv, w_out, xattn_norm_g, mem_norm_g, xattn_w_q, xattn_w_kv, xattn_w_o, ffn2_norm_g, ffn2_w_gate, ffn2_w_up, ffn2_w_down, final_norm_g):
    batch, seq, d = x.shape
    mem_len = mem.shape[1]
    depth = w_in.shape[0]
    h = x.reshape(batch * seq, d)
    mem2d = mem.reshape(batch * mem_len, d)
    row = lambda a: a.reshape(1, -1).astype(F32)
    b16 = lambda a: a.astype(BF16)

    splits = (ML_WIDTH, ML_WIDTH, ML_WIDTH, ML_HEADS, ML_HEADS, ML_WIDTH, DSA_HEADS * DSA_LATENT, DSA_LATENT,
              IDX_HEADS * IDX_DIM, IDX_DIM, IDX_HEADS)
    offs = [0]
    for s in splits:
        offs.append(offs[-1] + s)

    for l in range(depth):
        wi = w_in[l]
        cols = [wi[:, offs[i]:offs[i + 1]] for i in range(len(splits))]
        mq, mk, mv, mi, mf, mo, dq, dc, iq, ik, iw = cols
        w_gate = jnp.concatenate([mi, mf], axis=1)
        b_gate = jnp.concatenate([mlstm_i_bias[l], mlstm_f_bias[l]])
        p = {
            "f1g": row(ffn1_norm_g[l]), "f1wg": ffn1_w_gate[l], "f1wu": ffn1_w_up[l], "f1wd": ffn1_w_down[l],
            "f2g": row(ffn2_norm_g[l]), "f2wg": ffn2_w_gate[l], "f2wu": ffn2_w_up[l], "f2wd": ffn2_w_down[l],
            "mixg": row(mix_norm_g[l]),
            "wqk": b16(jnp.concatenate([mq, mk], axis=1)), "wv": b16(mv), "wo": b16(mo),
            "wgc": b16(jnp.pad(w_gate, ((0, 0), (0, 128 - 2 * ML_HEADS)))),
            "wgr": b16(jnp.pad(w_gate.T, ((0, 16 - 2 * ML_HEADS), (0, 0)))),
            "bgc": jnp.pad(b_gate, (0, 128 - 2 * ML_HEADS)).reshape(1, 128).astype(F32),
            "bgr": jnp.pad(b_gate, (0, 16 - 2 * ML_HEADS)).reshape(16, 1).astype(F32),
            "cw": mlstm_conv_w[l].astype(F32), "cb": row(mlstm_conv_b[l]),
            "wdq_t": b16(dq.T), "wiq_t": b16(iq.T), "wiw_t": b16(jnp.pad(iw.T, ((0, 16 - IDX_HEADS), (0, 0)))),
            "wdc": b16(dc), "kvg": row(dsa_kv_norm_g[l]),
            "wik": b16(jnp.pad(ik, ((0, 0), (0, 128 - IDX_DIM)))), "ikg": row(idx_k_norm_g[l]),
            "headg": row(mlstm_head_norm_g[l]),
            "wuv_t": b16(jnp.swapaxes(dsa_w_uv[l], 1, 2)),
            "woa": w_out[l][:ML_WIDTH], "wob": w_out[l][ML_WIDTH:],
            "xg": row(xattn_norm_g[l]), "memg": row(mem_norm_g[l]),
            "wq": xattn_w_q[l], "wkv": xattn_w_kv[l], "wxo": xattn_w_o[l],
        }
        fg = row(final_norm_g) if l == depth - 1 else None
        h = _layer(h, mem2d, p, batch, seq, mem_len, fg)
    return h.reshape(batch, seq, d)
```

```python
import functools

import jax
import jax.numpy as jnp
from jax import lax
from jax.experimental import pallas as pl
from jax.experimental.pallas import tpu as pltpu

F32 = jnp.float32
BF16 = jnp.bfloat16
I32 = jnp.int32
EPS = 1e-6

ML_HEADS = 4
ML_HEAD_DIM = 128
ML_WIDTH = ML_HEADS * ML_HEAD_DIM
ML_CONV = 4
DSA_HEADS = 8
DSA_LATENT = 128
DSA_HEAD_DIM = 64
DSA_WIDTH = DSA_HEADS * DSA_HEAD_DIM
IDX_HEADS = 8
IDX_DIM = 64
INDEX_TOPK = 256
XA_HEADS = 4

FFN_TM = 512
FFN_FC = 256
PROJ_TM = 1024
ML_CHUNK = 128
ML_SEQS = 2
DSA_TQ = 256
MIX_TM = 512

CKVT_ROWS = DSA_LATENT + 16

VMEM_LIMIT = 56 * 1024 * 1024
VMEM_SKEW = 128
INT_MIN = -(2 ** 31)
NEG_INF = float("-inf")
CODE_NEG_INF = 0x007FFFFF
LOG2E = 1.4426950408889634


def _rmsnorm(x, g):
    return x * lax.rsqrt(jnp.mean(x * x, axis=-1, keepdims=True) + EPS) * g


def _sigmoid(x):
    return 1.0 / (1.0 + jnp.exp(-x))


def _dot(a, b):
    return jnp.dot(a, b, preferred_element_type=F32)


def _dot_nt(a, b):
    return lax.dot_general(a, b, (((1,), (1,)), ((), ())), preferred_element_type=F32)


def _full(shape):
    return pl.BlockSpec(shape, lambda *_: (0,) * len(shape))


def _resident(shape):
    return pl.BlockSpec(shape, lambda *_: (0,) * len(shape), pipeline_mode=pl.Buffered(1))


def _params(*sem):
    return pltpu.CompilerParams(dimension_semantics=sem, vmem_limit_bytes=VMEM_LIMIT)


def _ffn_kernel(x_ref, g_ref, wg_ref, wu_ref, wd_ref, fg_ref, o_ref, *, final_norm):
    x = x_ref[...]
    xn = _rmsnorm(x, g_ref[...]).astype(BF16)
    d_ff = wg_ref.shape[1]
    acc = jnp.zeros(x.shape, F32)
    for c in range(d_ff // FFN_FC):
        sl = slice(c * FFN_FC, (c + 1) * FFN_FC)
        gate = _dot(xn, wg_ref[:, sl].astype(BF16))
        up = _dot(xn, wu_ref[:, sl].astype(BF16))
        act = (gate * _sigmoid(gate) * up).astype(BF16)
        acc = acc + _dot(act, wd_ref[sl, :].astype(BF16))
    y = x + 0.5 * acc
    if final_norm:
        y = _rmsnorm(y, fg_ref[...])
    o_ref[...] = y


def _ffn(x, g, wg, wu, wd, fg, final_norm):
    t, d = x.shape
    d_ff = wg.shape[1]
    return pl.pallas_call(
        functools.partial(_ffn_kernel, final_norm=final_norm),
        grid=(t // FFN_TM,),
        in_specs=[
            pl.BlockSpec((FFN_TM, d), lambda i: (i, 0)),
            _full((1, d)),
            _resident((d, d_ff)),
            _resident((d, d_ff)),
            _resident((d_ff, d)),
            _full((1, d)),
        ],
        out_specs=pl.BlockSpec((FFN_TM, d), lambda i: (i, 0)),
        out_shape=jax.ShapeDtypeStruct((t, d), F32),
        compiler_params=_params("parallel"),
        name="ffn_final" if final_norm else "ffn",
    )(x, g, wg, wu, wd, fg)


def _proj_kernel(h_ref, g_ref, wqk_ref, wv_ref, wo_ref, wgc_ref, wgr_ref, bgc_ref, bgr_ref,
                 cw_ref, cb_ref, wdq_ref, wiq_ref, wiw_ref, wdc_ref, kvg_ref, wik_ref, ikg_ref,
                 q_ref, k_ref, v_ref, o_ref, gc_ref, gr_ref, dqt_ref, iqt_ref, wit_ref,
                 ckv_ref, ckvt_ref, ki_ref, zbuf, *, tiles_per_seq):
    tm = h_ref.shape[0]
    u = _rmsnorm(h_ref[...], g_ref[...]).astype(BF16)

    @pl.when(pl.program_id(0) % tiles_per_seq == 0)
    def _():
        zbuf[0:8, :] = jnp.zeros((8, zbuf.shape[1]), F32)

    zbuf[8:8 + tm, :] = _dot(u, wqk_ref[...])
    y = jnp.zeros((tm, zbuf.shape[1]), F32) + cb_ref[...]
    for j in range(ML_CONV):
        y = y + zbuf[5 + j:5 + j + tm, :] * cw_ref[j:j + 1, :]
    zbuf[0:8, :] = zbuf[tm:tm + 8, :]
    y = y * _sigmoid(y)
    q_ref[...] = y[:, :ML_WIDTH].astype(BF16)
    k_ref[...] = (y[:, ML_WIDTH:] * (ML_HEAD_DIM ** -0.5)).astype(BF16)
    v_ref[...] = _dot(u, wv_ref[...]).astype(BF16)
    o_ref[...] = _dot(u, wo_ref[...]).astype(BF16)

    def gates(z, idx):
        return jnp.where(idx < ML_HEADS, z, jnp.minimum(z, 0.0) - jnp.log(1.0 + jnp.exp(-jnp.abs(z))))

    zc = _dot(u, wgc_ref[...]) + bgc_ref[...]
    gc_ref[...] = gates(zc, lax.broadcasted_iota(I32, zc.shape, 1))
    zr = _dot_nt(wgr_ref[...], u) + bgr_ref[...]
    gr_ref[...] = gates(zr, lax.broadcasted_iota(I32, zr.shape, 0))

    dqt_ref[...] = (_dot_nt(wdq_ref[...], u) * (DSA_LATENT ** -0.5 * LOG2E)).astype(BF16)
    iqt_ref[...] = _dot_nt(wiq_ref[...], u).astype(BF16)
    wit_ref[...] = _dot_nt(wiw_ref[...], u) * (IDX_HEADS ** -0.5) * (IDX_DIM ** -0.5)
    ckv = _rmsnorm(_dot(u, wdc_ref[...]), kvg_ref[...])
    ckv_ref[...] = ckv.astype(BF16)
    ckvt_ref[0:DSA_LATENT, :] = ckv.T.astype(BF16)
    ckvt_ref[DSA_LATENT:, :] = (lax.broadcasted_iota(I32, (CKVT_ROWS - DSA_LATENT, tm), 0) == 0).astype(BF16)
    ik = _dot(u, wik_ref[...])[:, :IDX_DIM]
    ki_ref[...] = _rmsnorm(ik, ikg_ref[...]).astype(BF16)


def _proj(h1, g, w, seq):
    t, d = h1.shape
    tm = PROJ_TM
    row = lambda n: pl.BlockSpec((tm, n), lambda i: (i, 0))
    col = lambda n: pl.BlockSpec((n, tm), lambda i: (0, i))
    outs = [
        ("q", row(ML_WIDTH), (t, ML_WIDTH), BF16),
        ("k", row(ML_WIDTH), (t, ML_WIDTH), BF16),
        ("v", row(ML_WIDTH), (t, ML_WIDTH), BF16),
        ("o", row(ML_WIDTH), (t, ML_WIDTH), BF16),
        ("gc", row(128), (t, 128), F32),
        ("gr", col(16), (16, t), F32),
        ("dqt", col(DSA_HEADS * DSA_LATENT), (DSA_HEADS * DSA_LATENT, t), BF16),
        ("iqt", col(IDX_HEADS * IDX_DIM), (IDX_HEADS * IDX_DIM, t), BF16),
        ("wit", col(16), (16, t), F32),
        ("ckv", row(DSA_LATENT), (t, DSA_LATENT), BF16),
        ("ckvt", col(CKVT_ROWS), (CKVT_ROWS, t), BF16),
        ("ki", row(IDX_DIM), (t, IDX_DIM), BF16),
    ]
    ins = [h1, g, w["wqk"], w["wv"], w["wo"], w["wgc"], w["wgr"], w["bgc"], w["bgr"], w["cw"], w["cb"],
           w["wdq_t"], w["wiq_t"], w["wiw_t"], w["wdc"], w["kvg"], w["wik"], w["ikg"]]
    res = pl.pallas_call(
        functools.partial(_proj_kernel, tiles_per_seq=seq // tm),
        grid=(t // tm,),
        in_specs=[row(d)] + [_resident(a.shape) for a in ins[1:]],
        out_specs=[o[1] for o in outs],
        out_shape=[jax.ShapeDtypeStruct(o[2], o[3]) for o in outs],
        scratch_shapes=[pltpu.VMEM((tm + 8, 2 * ML_WIDTH), F32)],
        compiler_params=_params("arbitrary"),
        name="proj",
    )(*ins)
    return {o[0]: r for o, r in zip(outs, res)}


def _split3(x):
    hi = x.astype(BF16)
    r1 = x - hi.astype(F32)
    mid = r1.astype(BF16)
    lo = (r1 - mid.astype(F32)).astype(BF16)
    return hi, mid, lo


def _mlstm_kernel(q_ref, k_ref, v_ref, o_ref, gc_ref, gr_ref, hg_ref, y_ref, c_ref, m_ref, *, seq):
    L = ML_CHUNK
    d = ML_HEAD_DIM
    c_ref[...] = jnp.zeros(c_ref.shape, F32)
    m_ref[...] = jnp.zeros(m_ref.shape, F32)

    ri = lax.broadcasted_iota(I32, (L, L), 0)
    ci = lax.broadcasted_iota(I32, (L, L), 1)
    causal = ci <= ri
    tri_lo = causal.astype(BF16)
    tri_up = (ri <= ci).astype(BF16)
    ones_col = (lax.broadcasted_iota(I32, (L, d), 1) == 0).astype(F32)

    nch = ML_SEQS * ML_HEADS
    causal_all = jnp.concatenate([causal] * nch, axis=0)
    stack = lambda parts: jnp.concatenate(parts, axis=0)
    col_of = lambda x4: stack([jnp.broadcast_to(x4[j:j + 1, :], (L, 1)) for j in range(nch)])
    gain = stack([jnp.broadcast_to(hg_ref[:, (j % ML_HEADS) * d:(j % ML_HEADS + 1) * d], (L, d)) for j in range(nch)])

    def chunk(c, carry):
        r0s = [pl.multiple_of(sq * seq + c * L, L) for sq in range(ML_SEQS)]
        hsl = lambda j: slice((j % ML_HEADS) * d, (j % ML_HEADS + 1) * d)
        rws = lambda j: pl.ds(r0s[j // ML_HEADS], L)
        bcs, brs, gcs, grs = [], [], [], []
        for sq in range(ML_SEQS):
            gc = gc_ref[pl.ds(r0s[sq], L), :]
            gr = gr_ref[:, pl.ds(r0s[sq], L)]
            bcs.append(sum(_dot(tri_lo, p) for p in _split3(gc)))
            brs.append(sum(_dot(p, tri_up) for p in _split3(gr)))
            gcs.append(gc)
            grs.append(gr)
        b_col = stack([bcs[j // ML_HEADS][:, ML_HEADS + j % ML_HEADS:ML_HEADS + j % ML_HEADS + 1] for j in range(nch)])
        i_col = stack([gcs[j // ML_HEADS][:, j % ML_HEADS:j % ML_HEADS + 1] for j in range(nch)])
        b_row4 = stack([brs[sq][ML_HEADS:2 * ML_HEADS, :] for sq in range(ML_SEQS)])
        i_row4 = stack([grs[sq][0:ML_HEADS, :] for sq in range(ML_SEQS)])
        u_row4 = i_row4 - b_row4
        u_blk = stack([jnp.broadcast_to(u_row4[j:j + 1, :], (L, L)) for j in range(nch)])
        m_prev4 = m_ref[0:nch, 0:1]
        logw = jnp.where(causal_all, b_col + u_blk, NEG_INF)
        inter = b_col + col_of(m_prev4)
        mj = jnp.maximum(inter, jnp.max(logw, axis=1, keepdims=True))
        w = jnp.exp(logw - mj)
        a = jnp.exp(inter - mj)
        qs = [q_ref[rws(j), hsl(j)] for j in range(nch)]
        ks = [k_ref[rws(j), hsl(j)] for j in range(nch)]
        vaugs = [jnp.concatenate([v_ref[rws(j), hsl(j)].astype(F32), ones_col], axis=1) for j in range(nch)]
        sqk = (stack([_dot_nt(qs[j], ks[j]) for j in range(nch)]) * w).astype(BF16)
        cts = [c_ref[j] for j in range(nch)]
        tot = a * stack([_dot(qs[j], cts[j].astype(BF16)) for j in range(nch)]) + stack(
            [_dot(sqk[j * L:(j + 1) * L, :], vaugs[j].astype(BF16)) for j in range(nch)])
        hh = tot[:, :d] / jnp.maximum(jnp.abs(tot[:, d:d + 1]), jnp.exp(-mj))
        gate = _sigmoid(stack([o_ref[rws(j), hsl(j)] for j in range(nch)]).astype(F32))
        y = (_rmsnorm(hh, gain) * gate).astype(BF16)
        for j in range(nch):
            y_ref[rws(j), hsl(j)] = y[j * L:(j + 1) * L, :]
        b_last4 = b_row4[:, L - 1:L]
        g_row4 = b_last4 + u_row4
        m_new4 = jnp.maximum(b_last4 + m_prev4, jnp.max(g_row4, axis=1, keepdims=True))
        dec4 = jnp.exp(b_last4 + m_prev4 - m_new4)
        wg = jnp.exp(col_of(b_last4 - m_new4) - b_col + i_col)
        for j in range(nch):
            kt = ks[j].astype(F32).T.astype(BF16)
            wv = (wg[j * L:(j + 1) * L, :] * vaugs[j]).astype(BF16)
            c_ref[j] = dec4[j:j + 1, :] * cts[j] + _dot(kt, wv)
        m_ref[0:nch, :] = jnp.broadcast_to(m_new4, (nch, m_ref.shape[1]))
        return carry

    lax.fori_loop(0, seq // L, chunk, 0)


def _mlstm(p, head_g, batch, seq):
    t = batch * seq
    rows = ML_SEQS * seq
    row = lambda n: pl.BlockSpec((rows, n), lambda b: (b, 0))
    return pl.pallas_call(
        functools.partial(_mlstm_kernel, seq=seq),
        grid=(batch // ML_SEQS,),
        in_specs=[row(ML_WIDTH), row(ML_WIDTH), row(ML_WIDTH), row(ML_WIDTH), row(128),
                  pl.BlockSpec((16, rows), lambda b: (0, b)), _full((1, ML_WIDTH))],
        out_specs=row(ML_WIDTH),
        out_shape=jax.ShapeDtypeStruct((t, ML_WIDTH), BF16),
        scratch_shapes=[pltpu.VMEM((ML_SEQS * ML_HEADS, ML_HEAD_DIM, 2 * ML_HEAD_DIM), F32),
                        pltpu.VMEM((ML_SEQS * ML_HEADS, 128), F32)],
        compiler_params=_params("parallel"),
        name="mlstm",
    )(p["q"], p["k"], p["v"], p["o"], p["gc"], p["gr"], head_g)


def _float_of_code(code):
    key = code ^ jnp.int32(INT_MIN)
    bits = jnp.where(key < 0, key ^ jnp.int32(0x7FFFFFFF), key)
    return lax.bitcast_convert_type(bits, F32)


def _colmax8(x):
    parts = [x[8 * j:8 * j + 8, :] for j in range(x.shape[0] // 8)]
    while len(parts) > 1:
        parts = [jnp.maximum(parts[j], parts[j + 1]) for j in range(0, len(parts) - 1, 2)] + (
            [parts[-1]] if len(parts) % 2 else [])
    return parts[0]


def _colsum(x):
    groups = x.shape[0] // 8
    lanes = min(2, groups)
    parts = [x[8 * j:8 * j + 8, :] for j in range(lanes)]
    for j in range(lanes, groups):
        parts[j % lanes] = parts[j % lanes] + x[8 * j:8 * j + 8, :]
    while len(parts) > 1:
        parts = [parts[j] + parts[j + 1] for j in range(0, len(parts), 2)]
    return jnp.sum(parts[0], axis=0, keepdims=True)


def _colsum_bf16(x):
    groups = x.shape[0] // 16
    assert x.shape[0] % 16 == 0 and groups <= 2 * 256
    lanes = min(2, groups)
    parts = [x[16 * j:16 * j + 16, :] for j in range(lanes)]
    for j in range(lanes, groups):
        parts[j % lanes] = parts[j % lanes] + x[16 * j:16 * j + 16, :]
    tot = parts[0].astype(F32)
    for part in parts[1:]:
        tot = tot + part.astype(F32)
    return jnp.sum(tot, axis=0, keepdims=True)


def _dsa_kernel(iqt_ref, wit_ref, dqt_ref, ki_ref, ckv_ref, ckvt_ref, wuvt_ref, y_ref,
                iqa_ref, dqa_ref, score_ref, sb_ref, bias_ref, lg_ref, acc_ref, code_ref, yt_ref, m8_ref, *, topk, nq):
    tq = iqt_ref.shape[1]
    nh = DSA_HEADS
    nk = pl.program_id(1) + 1
    last = nk - 1
    rows = lambda c: pl.ds(pl.multiple_of(c * tq, tq), tq)
    hs = lambda h: slice(h * tq, (h + 1) * tq)
    dmat = lax.broadcasted_iota(I32, (tq, tq), 0) - lax.broadcasted_iota(I32, (tq, tq), 1)
    allowed = lambda c: dmat <= jnp.where(c < last, tq, 0)

    for h in range(nh):
        iqa_ref[:, hs(h)] = iqt_ref[h * IDX_DIM:(h + 1) * IDX_DIM, :]
        dqa_ref[:, hs(h)] = dqt_ref[h * DSA_LATENT:(h + 1) * DSA_LATENT, :]
    wi_all = jnp.concatenate([wit_ref[h:h + 1, :] for h in range(IDX_HEADS)], axis=1)

    def for_blocks(body):
        def quad(p, carry):
            for u in range(4):
                body(4 * p + u)
            return carry

        def pair():
            body(nk & ~3)
            body((nk & ~3) + 1)

        lax.fori_loop(0, lax.shift_right_logical(nk, 2), quad, 0)
        pl.when((nk & 2) != 0)(pair)
        pl.when((nk & 1) != 0)(lambda: body(last))

    def idx_block(c):
        w = wi_all * jnp.maximum(_dot(ki_ref[rows(c), :], iqa_ref[...]), 0.0)
        acc = w[:, hs(0)]
        for h in range(1, IDX_HEADS):
            acc = acc + w[:, hs(h)]
        sc = jnp.where(allowed(c), acc, NEG_INF)
        score_ref[rows(c), 0:tq] = sc
        sb_ref[rows(c), 0:tq] = sc.astype(BF16)

    for_blocks(idx_block)

    def search(n):
        if n * tq <= topk:
            code_ref[0:1, :] = jnp.full((1, tq), CODE_NEG_INF, I32)
            return

        def hi_pass(i, code):
            cand_code = code | lax.shift_left(jnp.int32(1), 15 - i)
            cand = _float_of_code(lax.shift_left(cand_code, 16)).astype(BF16)
            ind = jnp.where(sb_ref[0:n * tq, 0:tq] < cand, jnp.zeros((), BF16), jnp.ones((), BF16))
            return jnp.where(_colsum_bf16(ind) >= topk, cand_code, code)

        c1 = lax.fori_loop(0, 16, hi_pass, jnp.zeros((1, tq), I32))
        base = lax.shift_left(c1, 16) - jnp.int32(0x8001)

        def lo_pass(i, off):
            cand_off = off | lax.shift_left(jnp.int32(1), 16 - i)
            cand = _float_of_code(base + cand_off)
            cnt = _colsum(jnp.where(score_ref[0:n * tq, 0:tq] < cand, 0.0, 1.0))
            return jnp.where(cnt >= topk, cand_off, off)

        code_ref[0:1, :] = base + lax.fori_loop(0, 17, lo_pass, jnp.zeros((1, tq), I32))

    for n in range(1, nq + 1):
        pl.when(nk == n)(functools.partial(search, n))
    thr = _float_of_code(code_ref[0:1, :])

    def mask_block(c, carry):
        n_gt, n_eq = carry
        s = score_ref[rows(c), 0:tq]
        bias_ref[rows(c), 0:tq] = jnp.where((s >= thr) & allowed(c), 0.0, NEG_INF)
        n_gt = n_gt + _colsum(jnp.where(s > thr, 1.0, 0.0))
        n_eq = n_eq + _colsum(jnp.where(s == thr, 1.0, 0.0))
        return n_gt, n_eq

    zero = jnp.zeros((1, tq), F32)
    n_gt, n_eq = lax.fori_loop(0, nk, mask_block, (zero, zero))
    room = topk - n_gt

    @pl.when(jnp.max(n_eq - room) > 0.0)
    def _():
        lo = (dmat > 0).astype(BF16)

        def tie_block(c, before):
            s = score_ref[rows(c), 0:tq]
            ef = jnp.where(s == thr, 1.0, 0.0)
            rank = before + _dot(lo, ef.astype(BF16))
            keep = ((s > thr) | ((s == thr) & (rank < room))) & allowed(c)
            bias_ref[rows(c), 0:tq] = jnp.where(keep, 0.0, NEG_INF)
            return before + jnp.sum(ef, axis=0, keepdims=True)

        lax.fori_loop(0, nk, tie_block, zero)

    m8_ref[...] = jnp.full(m8_ref.shape, NEG_INF, F32)

    def logit_block(c):
        s_all = _dot(ckv_ref[rows(c), :], dqa_ref[...])
        b = bias_ref[rows(c), 0:tq]
        for h in range(nh):
            s = s_all[:, hs(h)] + b
            lg_ref[rows(c), hs(h)] = s
            m8_ref[:, hs(h)] = jnp.maximum(m8_ref[:, hs(h)], _colmax8(s))

    for_blocks(logit_block)
    m = jnp.max(m8_ref[...], axis=0, keepdims=True)
    acc_ref[...] = jnp.zeros(acc_ref.shape, F32)

    def prob_block(c):
        ckvt = ckvt_ref[:, rows(c)]
        for h in range(nh):
            p = jnp.exp2(lg_ref[rows(c), hs(h)] - m[:, hs(h)]).astype(BF16)
            acc_ref[h] += _dot(ckvt, p)

    for_blocks(prob_block)
    for h in range(nh):
        ot = acc_ref[h, 0:DSA_LATENT, :] / acc_ref[h, DSA_LATENT:DSA_LATENT + 1, :]
        yt_ref[h * DSA_HEAD_DIM:(h + 1) * DSA_HEAD_DIM, :] = _dot(wuvt_ref[h], ot.astype(BF16))
    y_ref[...] = yt_ref[...].T.astype(BF16)


def _dsa(p, wuv_t, batch, seq):
    t = batch * seq
    tq = DSA_TQ
    nq = seq // tq
    topk = min(INDEX_TOPK, seq // 4)
    qcol = lambda n: pl.BlockSpec((n, tq), lambda b, i: (0, b * nq + i))
    return pl.pallas_call(
        functools.partial(_dsa_kernel, topk=float(topk), nq=nq),
        grid=(batch, nq),
        in_specs=[qcol(IDX_HEADS * IDX_DIM), qcol(16), qcol(DSA_HEADS * DSA_LATENT),
                  pl.BlockSpec((seq, IDX_DIM), lambda b, i: (b, 0)),
                  pl.BlockSpec((seq, DSA_LATENT), lambda b, i: (b, 0)),
                  pl.BlockSpec((CKVT_ROWS, seq), lambda b, i: (0, b)),
                  _full(wuv_t.shape)],
        out_specs=pl.BlockSpec((tq, DSA_WIDTH), lambda b, i: (b * nq + i, 0)),
        out_shape=jax.ShapeDtypeStruct((t, DSA_WIDTH), BF16),
        scratch_shapes=[pltpu.VMEM((IDX_DIM, IDX_HEADS * tq), BF16), pltpu.VMEM((DSA_LATENT, DSA_HEADS * tq), BF16),
                        pltpu.VMEM((seq, tq + VMEM_SKEW), F32), pltpu.VMEM((seq, tq + VMEM_SKEW), BF16),
                        pltpu.VMEM((seq, tq + VMEM_SKEW), F32), pltpu.VMEM((seq, DSA_HEADS * tq + VMEM_SKEW), F32), pltpu.VMEM((DSA_HEADS, CKVT_ROWS, tq), F32),
                        pltpu.VMEM((8, tq), I32), pltpu.VMEM((DSA_WIDTH, tq), F32),
                        pltpu.VMEM((8, DSA_HEADS * tq), F32)],
        compiler_params=_params("parallel", "arbitrary"),
        name="dsa",
    )(p["iqt"], p["wit"], p["dqt"], p["ki"], p["ckv"], p["ckvt"], wuv_t)


def _memkv_kernel(m_ref, g_ref, w_ref, o_ref):
    o_ref[...] = _dot(_rmsnorm(m_ref[...], g_ref[...]).astype(BF16), w_ref[...].astype(BF16)).astype(BF16)


def _memkv(mem2d, g, w_kv, mem_len):
    rows, d = mem2d.shape
    return pl.pallas_call(
        _memkv_kernel,
        grid=(rows // mem_len,),
        in_specs=[pl.BlockSpec((mem_len, d), lambda b: (b, 0)), _full((1, d)), _resident(w_kv.shape)],
        out_specs=pl.BlockSpec((mem_len, w_kv.shape[1]), lambda b: (b, 0)),
        out_shape=jax.ShapeDtypeStruct((rows, w_kv.shape[1]), BF16),
        compiler_params=_params("parallel"),
        name="memkv",
    )(mem2d, g, w_kv)


def _mixout_kernel(h_ref, yml_ref, ydsa_ref, woa_ref, wob_ref, g_ref, wq_ref, kv_ref, wo_ref, o_ref):
    d = h_ref.shape[1]
    hd = d // XA_HEADS
    h2 = h_ref[...] + _dot(yml_ref[...], woa_ref[...].astype(BF16)) + _dot(ydsa_ref[...], wob_ref[...].astype(BF16))
    q = _dot(_rmsnorm(h2, g_ref[...]).astype(BF16), wq_ref[...].astype(BF16)).astype(BF16)
    heads = []
    for h in range(XA_HEADS):
        kh = kv_ref[:, h * hd:(h + 1) * hd]
        vh = kv_ref[:, d + h * hd:d + (h + 1) * hd]
        lg = _dot_nt(q[:, h * hd:(h + 1) * hd], kh) * (hd ** -0.5)
        p = jnp.exp(lg - jnp.max(lg, axis=1, keepdims=True))
        denom = jnp.sum(p, axis=1, keepdims=True)
        heads.append((_dot(p.astype(BF16), vh) / denom).astype(BF16))
    o_ref[...] = h2 + _dot(jnp.concatenate(heads, axis=1), wo_ref[...].astype(BF16))


def _mixout(h1, yml, ydsa, w, kv, batch, seq, mem_len):
    t, d = h1.shape
    tm = MIX_TM
    nt = seq // tm
    row = lambda n: pl.BlockSpec((tm, n), lambda b, j: (b * nt + j, 0))
    return pl.pallas_call(
        _mixout_kernel,
        grid=(batch, nt),
        in_specs=[row(d), row(ML_WIDTH), row(DSA_WIDTH), _resident(w["woa"].shape), _resident(w["wob"].shape),
                  _full((1, d)), _resident(w["wq"].shape),
                  pl.BlockSpec((mem_len, 2 * d), lambda b, j: (b, 0)), _resident(w["wxo"].shape)],
        out_specs=row(d),
        out_shape=jax.ShapeDtypeStruct((t, d), F32),
        compiler_params=_params("parallel", "parallel"),
        name="mixout",
    )(h1, yml, ydsa, w["woa"], w["wob"], w["xg"], w["wq"], kv, w["wxo"])


def _layer(h, mem2d, p, batch, seq, mem_len, final_g):
    h1 = _ffn(h, p["f1g"], p["f1wg"], p["f1wu"], p["f1wd"], p["f1g"], final_norm=False)
    pr = _proj(h1, p["mixg"], p, seq)
    yml = _mlstm(pr, p["headg"], batch, seq)
    ydsa = _dsa(pr, p["wuv_t"], batch, seq)
    kv = _memkv(mem2d, p["memg"], p["wkv"], mem_len)
    h3 = _mixout(h1, yml, ydsa, p, kv, batch, seq, mem_len)
    fg = p["f2g"] if final_g is None else final_g
    return _ffn(h3, p["f2g"], p["f2wg"], p["f2wu"], p["f2wd"], fg, final_norm=final_g is not None)


def kernel(x, mem, ffn1_norm_g, ffn1_w_gate, ffn1_w_up, ffn1_w_down, mix_norm_g, w_in, mlstm_conv_w, mlstm_conv_b, mlstm_i_bias, mlstm_f_bias, mlstm_head_norm_g, dsa_kv_norm_g, idx_k_norm_g, dsa_w_uv, w_out, xattn_norm_g, mem_norm_g, xattn_w_q, xattn_w_kv, xattn_w_o, ffn2_norm_g, ffn2_w_gate, ffn2_w_up, ffn2_w_down, final_norm_g):
    batch, seq, d = x.shape
    mem_len = mem.shape[1]
    depth = w_in.shape[0]
    h = x.reshape(batch * seq, d)
    mem2d = mem.reshape(batch * mem_len, d)
    row = lambda a: a.reshape(1, -1).astype(F32)
    b16 = lambda a: a.astype(BF16)

    splits = (ML_WIDTH, ML_WIDTH, ML_WIDTH, ML_HEADS, ML_HEADS, ML_WIDTH, DSA_HEADS * DSA_LATENT, DSA_LATENT,
              IDX_HEADS * IDX_DIM, IDX_DIM, IDX_HEADS)
    offs = [0]
    for s in splits:
        offs.append(offs[-1] + s)

    for l in range(depth):
        wi = w_in[l]
        cols = [wi[:, offs[i]:offs[i + 1]] for i in range(len(splits))]
        mq, mk, mv, mi, mf, mo, dq, dc, iq, ik, iw = cols
        w_gate = jnp.concatenate([mi, mf], axis=1)
        b_gate = jnp.concatenate([mlstm_i_bias[l], mlstm_f_bias[l]])
        p = {
            "f1g": row(ffn1_norm_g[l]), "f1wg": ffn1_w_gate[l], "f1wu": ffn1_w_up[l], "f1wd": ffn1_w_down[l],
            "f2g": row(ffn2_norm_g[l]), "f2wg": ffn2_w_gate[l], "f2wu": ffn2_w_up[l], "f2wd": ffn2_w_down[l],
            "mixg": row(mix_norm_g[l]),
            "wqk": b16(jnp.concatenate([mq, mk], axis=1)), "wv": b16(mv), "wo": b16(mo),
            "wgc": b16(jnp.pad(w_gate, ((0, 0), (0, 128 - 2 * ML_HEADS)))),
            "wgr": b16(jnp.pad(w_gate.T, ((0, 16 - 2 * ML_HEADS), (0, 0)))),
            "bgc": jnp.pad(b_gate, (0, 128 - 2 * ML_HEADS)).reshape(1, 128).astype(F32),
            "bgr": jnp.pad(b_gate, (0, 16 - 2 * ML_HEADS)).reshape(16, 1).astype(F32),
            "cw": mlstm_conv_w[l].astype(F32), "cb": row(mlstm_conv_b[l]),
            "wdq_t": b16(dq.T), "wiq_t": b16(iq.T), "wiw_t": b16(jnp.pad(iw.T, ((0, 16 - IDX_HEADS), (0, 0)))),
            "wdc": b16(dc), "kvg": row(dsa_kv_norm_g[l]),
            "wik": b16(jnp.pad(ik, ((0, 0), (0, 128 - IDX_DIM)))), "ikg": row(idx_k_norm_g[l]),
            "headg": row(mlstm_head_norm_g[l]),
            "wuv_t": b16(jnp.swapaxes(dsa_w_uv[l], 1, 2)),
            "woa": w_out[l][:ML_WIDTH], "wob": w_out[l][ML_WIDTH:],
            "xg": row(xattn_norm_g[l]), "memg": row(mem_norm_g[l]),
            "wq": xattn_w_q[l], "wkv": xattn_w_kv[l], "wxo": xattn_w_o[l],
        }
        fg = row(final_norm_g) if l == depth - 1 else None
        h = _layer(h, mem2d, p, batch, seq, mem_len, fg)
    return h.reshape(batch, seq, d)
```

```python
import functools

import jax
import jax.numpy as jnp
from jax import lax
from jax.experimental import pallas as pl
from jax.experimental.pallas import tpu as pltpu

F32 = jnp.float32
BF16 = jnp.bfloat16
I32 = jnp.int32
EPS = 1e-6

ML_HEADS = 4
ML_HEAD_DIM = 128
ML_WIDTH = ML_HEADS * ML_HEAD_DIM
ML_CONV = 4
DSA_HEADS = 8
DSA_LATENT = 128
DSA_HEAD_DIM = 64
DSA_WIDTH = DSA_HEADS * DSA_HEAD_DIM
IDX_HEADS = 8
IDX_DIM = 64
INDEX_TOPK = 256
XA_HEADS = 4

FFN_TM = 512
FFN_FC = 256
PROJ_TM = 1024
ML_CHUNK = 128
ML_SEQS = 2
DSA_TQ = 256
MIX_TM = 1024

CKVT_ROWS = DSA_LATENT + 16

VMEM_LIMIT = 56 * 1024 * 1024
INT_MIN = -(2 ** 31)
NEG_INF = float("-inf")
CODE_NEG_INF = 0x007FFFFF
LOG2E = 1.4426950408889634


def _rmsnorm(x, g):
    return x * lax.rsqrt(jnp.mean(x * x, axis=-1, keepdims=True) + EPS) * g


def _sigmoid(x):
    return 1.0 / (1.0 + jnp.exp(-x))


def _dot(a, b):
    return jnp.dot(a, b, preferred_element_type=F32)


def _dot_nt(a, b):
    return lax.dot_general(a, b, (((1,), (1,)), ((), ())), preferred_element_type=F32)


def _full(shape):
    return pl.BlockSpec(shape, lambda *_: (0,) * len(shape))


def _resident(shape):
    return pl.BlockSpec(shape, lambda *_: (0,) * len(shape), pipeline_mode=pl.Buffered(1))


def _params(*sem):
    return pltpu.CompilerParams(dimension_semantics=sem, vmem_limit_bytes=VMEM_LIMIT)


def _ffn_kernel(x_ref, g_ref, wg_ref, wu_ref, wd_ref, fg_ref, o_ref, *, final_norm):
    x = x_ref[...]
    xn = _rmsnorm(x, g_ref[...]).astype(BF16)
    d_ff = wg_ref.shape[1]
    acc = jnp.zeros(x.shape, F32)
    for c in range(d_ff // FFN_FC):
        sl = slice(c * FFN_FC, (c + 1) * FFN_FC)
        gate = _dot(xn, wg_ref[:, sl].astype(BF16))
        up = _dot(xn, wu_ref[:, sl].astype(BF16))
        act = (gate * _sigmoid(gate) * up).astype(BF16)
        acc = acc + _dot(act, wd_ref[sl, :].astype(BF16))
    y = x + 0.5 * acc
    if final_norm:
        y = _rmsnorm(y, fg_ref[...])
    o_ref[...] = y


def _ffn(x, g, wg, wu, wd, fg, final_norm):
    t, d = x.shape
    d_ff = wg.shape[1]
    return pl.pallas_call(
        functools.partial(_ffn_kernel, final_norm=final_norm),
        grid=(t // FFN_TM,),
        in_specs=[
            pl.BlockSpec((FFN_TM, d), lambda i: (i, 0)),
            _full((1, d)),
            _resident((d, d_ff)),
            _resident((d, d_ff)),
            _resident((d_ff, d)),
            _full((1, d)),
        ],
        out_specs=pl.BlockSpec((FFN_TM, d), lambda i: (i, 0)),
        out_shape=jax.ShapeDtypeStruct((t, d), F32),
        compiler_params=_params("parallel"),
        name="ffn_final" if final_norm else "ffn",
    )(x, g, wg, wu, wd, fg)


def _proj_kernel(h_ref, g_ref, wqk_ref, wv_ref, wo_ref, wgc_ref, wgr_ref, bgc_ref, bgr_ref,
                 cw_ref, cb_ref, wdq_ref, wiq_ref, wiw_ref, wdc_ref, kvg_ref, wik_ref, ikg_ref,
                 q_ref, k_ref, v_ref, o_ref, gc_ref, gr_ref, dqt_ref, iqt_ref, wit_ref,
                 ckv_ref, ckvt_ref, ki_ref, zbuf, *, tiles_per_seq):
    tm = h_ref.shape[0]
    u = _rmsnorm(h_ref[...], g_ref[...]).astype(BF16)

    @pl.when(pl.program_id(0) % tiles_per_seq == 0)
    def _():
        zbuf[0:8, :] = jnp.zeros((8, zbuf.shape[1]), F32)

    zbuf[8:8 + tm, :] = _dot(u, wqk_ref[...])
    y = jnp.zeros((tm, zbuf.shape[1]), F32) + cb_ref[...]
    for j in range(ML_CONV):
        y = y + zbuf[5 + j:5 + j + tm, :] * cw_ref[j:j + 1, :]
    zbuf[0:8, :] = zbuf[tm:tm + 8, :]
    y = y * _sigmoid(y)
    q_ref[...] = y[:, :ML_WIDTH].astype(BF16)
    k_ref[...] = (y[:, ML_WIDTH:] * (ML_HEAD_DIM ** -0.5)).astype(BF16)
    v_ref[...] = _dot(u, wv_ref[...]).astype(BF16)
    o_ref[...] = _dot(u, wo_ref[...]).astype(BF16)

    def gates(z, idx):
        return jnp.where(idx < ML_HEADS, z, jnp.minimum(z, 0.0) - jnp.log(1.0 + jnp.exp(-jnp.abs(z))))

    zc = _dot(u, wgc_ref[...]) + bgc_ref[...]
    gc_ref[...] = gates(zc, lax.broadcasted_iota(I32, zc.shape, 1))
    zr = _dot_nt(wgr_ref[...], u) + bgr_ref[...]
    gr_ref[...] = gates(zr, lax.broadcasted_iota(I32, zr.shape, 0))

    dqt_ref[...] = (_dot_nt(wdq_ref[...], u) * (DSA_LATENT ** -0.5 * LOG2E)).astype(BF16)
    iqt_ref[...] = _dot_nt(wiq_ref[...], u).astype(BF16)
    wit_ref[...] = _dot_nt(wiw_ref[...], u) * (IDX_HEADS ** -0.5) * (IDX_DIM ** -0.5)
    ckv = _rmsnorm(_dot(u, wdc_ref[...]), kvg_ref[...])
    ckv_ref[...] = ckv.astype(BF16)
    ckvt_ref[0:DSA_LATENT, :] = ckv.T.astype(BF16)
    ckvt_ref[DSA_LATENT:, :] = (lax.broadcasted_iota(I32, (CKVT_ROWS - DSA_LATENT, tm), 0) == 0).astype(BF16)
    ik = _dot(u, wik_ref[...])[:, :IDX_DIM]
    ki_ref[...] = _rmsnorm(ik, ikg_ref[...]).astype(BF16)


def _proj(h1, g, w, seq):
    t, d = h1.shape
    tm = PROJ_TM
    row = lambda n: pl.BlockSpec((tm, n), lambda i: (i, 0))
    col = lambda n: pl.BlockSpec((n, tm), lambda i: (0, i))
    outs = [
        ("q", row(ML_WIDTH), (t, ML_WIDTH), BF16),
        ("k", row(ML_WIDTH), (t, ML_WIDTH), BF16),
        ("v", row(ML_WIDTH), (t, ML_WIDTH), BF16),
        ("o", row(ML_WIDTH), (t, ML_WIDTH), BF16),
        ("gc", row(128), (t, 128), F32),
        ("gr", col(16), (16, t), F32),
        ("dqt", col(DSA_HEADS * DSA_LATENT), (DSA_HEADS * DSA_LATENT, t), BF16),
        ("iqt", col(IDX_HEADS * IDX_DIM), (IDX_HEADS * IDX_DIM, t), BF16),
        ("wit", col(16), (16, t), F32),
        ("ckv", row(DSA_LATENT), (t, DSA_LATENT), BF16),
        ("ckvt", col(CKVT_ROWS), (CKVT_ROWS, t), BF16),
        ("ki", row(IDX_DIM), (t, IDX_DIM), BF16),
    ]
    ins = [h1, g, w["wqk"], w["wv"], w["wo"], w["wgc"], w["wgr"], w["bgc"], w["bgr"], w["cw"], w["cb"],
           w["wdq_t"], w["wiq_t"], w["wiw_t"], w["wdc"], w["kvg"], w["wik"], w["ikg"]]
    res = pl.pallas_call(
        functools.partial(_proj_kernel, tiles_per_seq=seq // tm),
        grid=(t // tm,),
        in_specs=[row(d)] + [_resident(a.shape) for a in ins[1:]],
        out_specs=[o[1] for o in outs],
        out_shape=[jax.ShapeDtypeStruct(o[2], o[3]) for o in outs],
        scratch_shapes=[pltpu.VMEM((tm + 8, 2 * ML_WIDTH), F32)],
        compiler_params=_params("arbitrary"),
        name="proj",
    )(*ins)
    return {o[0]: r for o, r in zip(outs, res)}


def _split3(x):
    hi = x.astype(BF16)
    r1 = x - hi.astype(F32)
    mid = r1.astype(BF16)
    lo = (r1 - mid.astype(F32)).astype(BF16)
    return hi, mid, lo


def _mlstm_kernel(q_ref, k_ref, v_ref, o_ref, gc_ref, gr_ref, hg_ref, y_ref, c_ref, m_ref, *, seq):
    L = ML_CHUNK
    d = ML_HEAD_DIM
    c_ref[...] = jnp.zeros(c_ref.shape, F32)
    m_ref[...] = jnp.zeros(m_ref.shape, F32)

    ri = lax.broadcasted_iota(I32, (L, L), 0)
    ci = lax.broadcasted_iota(I32, (L, L), 1)
    causal = ci <= ri
    tri_lo = causal.astype(BF16)
    tri_up = (ri <= ci).astype(BF16)
    ones_col = (lax.broadcasted_iota(I32, (L, d), 1) == 0).astype(F32)

    nch = ML_SEQS * ML_HEADS
    causal_all = jnp.concatenate([causal] * nch, axis=0)
    stack = lambda parts: jnp.concatenate(parts, axis=0)
    col_of = lambda x4: stack([jnp.broadcast_to(x4[j:j + 1, :], (L, 1)) for j in range(nch)])
    gain = stack([jnp.broadcast_to(hg_ref[:, (j % ML_HEADS) * d:(j % ML_HEADS + 1) * d], (L, d)) for j in range(nch)])

    def chunk(c, carry):
        r0s = [pl.multiple_of(sq * seq + c * L, L) for sq in range(ML_SEQS)]
        hsl = lambda j: slice((j % ML_HEADS) * d, (j % ML_HEADS + 1) * d)
        rws = lambda j: pl.ds(r0s[j // ML_HEADS], L)
        bcs, brs, gcs, grs = [], [], [], []
        for sq in range(ML_SEQS):
            gc = gc_ref[pl.ds(r0s[sq], L), :]
            gr = gr_ref[:, pl.ds(r0s[sq], L)]
            bcs.append(sum(_dot(tri_lo, p) for p in _split3(gc)))
            brs.append(sum(_dot(p, tri_up) for p in _split3(gr)))
            gcs.append(gc)
            grs.append(gr)
        b_col = stack([bcs[j // ML_HEADS][:, ML_HEADS + j % ML_HEADS:ML_HEADS + j % ML_HEADS + 1] for j in range(nch)])
        i_col = stack([gcs[j // ML_HEADS][:, j % ML_HEADS:j % ML_HEADS + 1] for j in range(nch)])
        b_row4 = stack([brs[sq][ML_HEADS:2 * ML_HEADS, :] for sq in range(ML_SEQS)])
        i_row4 = stack([grs[sq][0:ML_HEADS, :] for sq in range(ML_SEQS)])
        u_row4 = i_row4 - b_row4
        u_blk = stack([jnp.broadcast_to(u_row4[j:j + 1, :], (L, L)) for j in range(nch)])
        m_prev4 = m_ref[0:nch, 0:1]
        logw = jnp.where(causal_all, b_col + u_blk, NEG_INF)
        inter = b_col + col_of(m_prev4)
        mj = jnp.maximum(inter, jnp.max(logw, axis=1, keepdims=True))
        w = jnp.exp(logw - mj)
        a = jnp.exp(inter - mj)
        qs = [q_ref[rws(j), hsl(j)] for j in range(nch)]
        ks = [k_ref[rws(j), hsl(j)] for j in range(nch)]
        vaugs = [jnp.concatenate([v_ref[rws(j), hsl(j)].astype(F32), ones_col], axis=1) for j in range(nch)]
        sqk = (stack([_dot_nt(qs[j], ks[j]) for j in range(nch)]) * w).astype(BF16)
        cts = [c_ref[j] for j in range(nch)]
        tot = a * stack([_dot(qs[j], cts[j].astype(BF16)) for j in range(nch)]) + stack(
            [_dot(sqk[j * L:(j + 1) * L, :], vaugs[j].astype(BF16)) for j in range(nch)])
        hh = tot[:, :d] / jnp.maximum(jnp.abs(tot[:, d:d + 1]), jnp.exp(-mj))
        gate = _sigmoid(stack([o_ref[rws(j), hsl(j)] for j in range(nch)]).astype(F32))
        y = (_rmsnorm(hh, gain) * gate).astype(BF16)
        for j in range(nch):
            y_ref[rws(j), hsl(j)] = y[j * L:(j + 1) * L, :]
        b_last4 = b_row4[:, L - 1:L]
        g_row4 = b_last4 + u_row4
        m_new4 = jnp.maximum(b_last4 + m_prev4, jnp.max(g_row4, axis=1, keepdims=True))
        dec4 = jnp.exp(b_last4 + m_prev4 - m_new4)
        wg = jnp.exp(col_of(b_last4 - m_new4) - b_col + i_col)
        for j in range(nch):
            kt = ks[j].astype(F32).T.astype(BF16)
            wv = (wg[j * L:(j + 1) * L, :] * vaugs[j]).astype(BF16)
            c_ref[j] = dec4[j:j + 1, :] * cts[j] + _dot(kt, wv)
        m_ref[0:nch, :] = jnp.broadcast_to(m_new4, (nch, m_ref.shape[1]))
        return carry

    lax.fori_loop(0, seq // L, chunk, 0)


def _mlstm(p, head_g, batch, seq):
    t = batch * seq
    rows = ML_SEQS * seq
    row = lambda n: pl.BlockSpec((rows, n), lambda b: (b, 0))
    return pl.pallas_call(
        functools.partial(_mlstm_kernel, seq=seq),
        grid=(batch // ML_SEQS,),
        in_specs=[row(ML_WIDTH), row(ML_WIDTH), row(ML_WIDTH), row(ML_WIDTH), row(128),
                  pl.BlockSpec((16, rows), lambda b: (0, b)), _full((1, ML_WIDTH))],
        out_specs=row(ML_WIDTH),
        out_shape=jax.ShapeDtypeStruct((t, ML_WIDTH), BF16),
        scratch_shapes=[pltpu.VMEM((ML_SEQS * ML_HEADS, ML_HEAD_DIM, 2 * ML_HEAD_DIM), F32),
                        pltpu.VMEM((ML_SEQS * ML_HEADS, 128), F32)],
        compiler_params=_params("parallel"),
        name="mlstm",
    )(p["q"], p["k"], p["v"], p["o"], p["gc"], p["gr"], head_g)


def _float_of_code(code):
    key = code ^ jnp.int32(INT_MIN)
    bits = jnp.where(key < 0, key ^ jnp.int32(0x7FFFFFFF), key)
    return lax.bitcast_convert_type(bits, F32)


def _colmax8(x):
    parts = [x[8 * j:8 * j + 8, :] for j in range(x.shape[0] // 8)]
    while len(parts) > 1:
        parts = [jnp.maximum(parts[j], parts[j + 1]) for j in range(0, len(parts) - 1, 2)] + (
            [parts[-1]] if len(parts) % 2 else [])
    return parts[0]


def _colsum(x):
    groups = x.shape[0] // 8
    lanes = min(2, groups)
    parts = [x[8 * j:8 * j + 8, :] for j in range(lanes)]
    for j in range(lanes, groups):
        parts[j % lanes] = parts[j % lanes] + x[8 * j:8 * j + 8, :]
    while len(parts) > 1:
        parts = [parts[j] + parts[j + 1] for j in range(0, len(parts), 2)]
    return jnp.sum(parts[0], axis=0, keepdims=True)


def _colsum_bf16(x):
    groups = x.shape[0] // 16
    assert x.shape[0] % 16 == 0 and groups <= 2 * 256
    lanes = min(2, groups)
    parts = [x[16 * j:16 * j + 16, :] for j in range(lanes)]
    for j in range(lanes, groups):
        parts[j % lanes] = parts[j % lanes] + x[16 * j:16 * j + 16, :]
    tot = parts[0].astype(F32)
    for part in parts[1:]:
        tot = tot + part.astype(F32)
    return jnp.sum(tot, axis=0, keepdims=True)


def _dsa_kernel(iqt_ref, wit_ref, dqt_ref, ki_ref, ckv_ref, ckvt_ref, wuvt_ref, y_ref,
                iqa_ref, dqa_ref, score_ref, sb_ref, bias_ref, lg_ref, acc_ref, code_ref, yt_ref, m8_ref, *, topk, nq):
    tq = iqt_ref.shape[1]
    nh = DSA_HEADS
    nk = pl.program_id(1) + 1
    last = nk - 1
    rows = lambda c: pl.ds(pl.multiple_of(c * tq, tq), tq)
    hs = lambda h: slice(h * tq, (h + 1) * tq)
    dmat = lax.broadcasted_iota(I32, (tq, tq), 0) - lax.broadcasted_iota(I32, (tq, tq), 1)
    allowed = lambda c: dmat <= jnp.where(c < last, tq, 0)

    for h in range(nh):
        iqa_ref[:, hs(h)] = iqt_ref[h * IDX_DIM:(h + 1) * IDX_DIM, :]
        dqa_ref[:, hs(h)] = dqt_ref[h * DSA_LATENT:(h + 1) * DSA_LATENT, :]
    wi_all = jnp.concatenate([wit_ref[h:h + 1, :] for h in range(IDX_HEADS)], axis=1)

    def for_blocks(body):
        def quad(p, carry):
            for u in range(4):
                body(4 * p + u)
            return carry

        def pair():
            body(nk & ~3)
            body((nk & ~3) + 1)

        lax.fori_loop(0, lax.shift_right_logical(nk, 2), quad, 0)
        pl.when((nk & 2) != 0)(pair)
        pl.when((nk & 1) != 0)(lambda: body(last))

    def idx_block(c):
        w = wi_all * jnp.maximum(_dot(ki_ref[rows(c), :], iqa_ref[...]), 0.0)
        acc = w[:, hs(0)]
        for h in range(1, IDX_HEADS):
            acc = acc + w[:, hs(h)]
        sc = jnp.where(allowed(c), acc, NEG_INF)
        score_ref[rows(c), :] = sc
        sb_ref[rows(c), :] = sc.astype(BF16)

    for_blocks(idx_block)

    def search(n):
        if n * tq <= topk:
            code_ref[0:1, :] = jnp.full((1, tq), CODE_NEG_INF, I32)
            return

        def hi_pass(i, code):
            cand_code = code | lax.shift_left(jnp.int32(1), 15 - i)
            cand = _float_of_code(lax.shift_left(cand_code, 16)).astype(BF16)
            ind = jnp.where(sb_ref[0:n * tq, :] < cand, jnp.zeros((), BF16), jnp.ones((), BF16))
            return jnp.where(_colsum_bf16(ind) >= topk, cand_code, code)

        c1 = lax.fori_loop(0, 16, hi_pass, jnp.zeros((1, tq), I32), unroll=2)
        base = lax.shift_left(c1, 16) - jnp.int32(0x8001)

        def lo_pass(i, off):
            cand_off = off | lax.shift_left(jnp.int32(1), 16 - i)
            cand = _float_of_code(base + cand_off)
            cnt = _colsum(jnp.where(score_ref[0:n * tq, :] < cand, 0.0, 1.0))
            return jnp.where(cnt >= topk, cand_off, off)

        off = lax.fori_loop(0, 16, lo_pass, jnp.zeros((1, tq), I32), unroll=2)
        code_ref[0:1, :] = base + lo_pass(16, off)

    for n in range(1, nq + 1):
        pl.when(nk == n)(functools.partial(search, n))
    thr = _float_of_code(code_ref[0:1, :])

    def mask_block(c, carry):
        n_gt, n_eq = carry
        s = score_ref[rows(c), :]
        bias_ref[rows(c), :] = jnp.where((s >= thr) & allowed(c), 0.0, NEG_INF)
        n_gt = n_gt + _colsum(jnp.where(s > thr, 1.0, 0.0))
        n_eq = n_eq + _colsum(jnp.where(s == thr, 1.0, 0.0))
        return n_gt, n_eq

    zero = jnp.zeros((1, tq), F32)
    n_gt, n_eq = lax.fori_loop(0, nk, mask_block, (zero, zero))
    room = topk - n_gt

    @pl.when(jnp.max(n_eq - room) > 0.0)
    def _():
        lo = (dmat > 0).astype(BF16)

        def tie_block(c, before):
            s = score_ref[rows(c), :]
            ef = jnp.where(s == thr, 1.0, 0.0)
            rank = before + _dot(lo, ef.astype(BF16))
            keep = ((s > thr) | ((s == thr) & (rank < room))) & allowed(c)
            bias_ref[rows(c), :] = jnp.where(keep, 0.0, NEG_INF)
            return before + jnp.sum(ef, axis=0, keepdims=True)

        lax.fori_loop(0, nk, tie_block, zero)

    m8_ref[...] = jnp.full(m8_ref.shape, NEG_INF, F32)

    def logit_block(c):
        s_all = _dot(ckv_ref[rows(c), :], dqa_ref[...])
        b = bias_ref[rows(c), :]
        for h in range(nh):
            s = s_all[:, hs(h)] + b
            lg_ref[rows(c), hs(h)] = s
            m8_ref[:, hs(h)] = jnp.maximum(m8_ref[:, hs(h)], _colmax8(s))

    for_blocks(logit_block)
    m = jnp.max(m8_ref[...], axis=0, keepdims=True)
    acc_ref[...] = jnp.zeros(acc_ref.shape, F32)

    def prob_block(c):
        ckvt = ckvt_ref[:, rows(c)]
        for h in range(nh):
            p = jnp.exp2(lg_ref[rows(c), hs(h)] - m[:, hs(h)]).astype(BF16)
            acc_ref[h] += _dot(ckvt, p)

    for_blocks(prob_block)
    for h in range(nh):
        ot = acc_ref[h, 0:DSA_LATENT, :] / acc_ref[h, DSA_LATENT:DSA_LATENT + 1, :]
        yt_ref[h * DSA_HEAD_DIM:(h + 1) * DSA_HEAD_DIM, :] = _dot(wuvt_ref[h], ot.astype(BF16))
    y_ref[...] = yt_ref[...].T.astype(BF16)


def _dsa(p, wuv_t, batch, seq):
    t = batch * seq
    tq = DSA_TQ
    nq = seq // tq
    topk = min(INDEX_TOPK, seq // 4)
    qcol = lambda n: pl.BlockSpec((n, tq), lambda b, i: (0, b * nq + i))
    return pl.pallas_call(
        functools.partial(_dsa_kernel, topk=float(topk), nq=nq),
        grid=(batch, nq),
        in_specs=[qcol(IDX_HEADS * IDX_DIM), qcol(16), qcol(DSA_HEADS * DSA_LATENT),
                  pl.BlockSpec((seq, IDX_DIM), lambda b, i: (b, 0)),
                  pl.BlockSpec((seq, DSA_LATENT), lambda b, i: (b, 0)),
                  pl.BlockSpec((CKVT_ROWS, seq), lambda b, i: (0, b)),
                  _full(wuv_t.shape)],
        out_specs=pl.BlockSpec((tq, DSA_WIDTH), lambda b, i: (b * nq + i, 0)),
        out_shape=jax.ShapeDtypeStruct((t, DSA_WIDTH), BF16),
        scratch_shapes=[pltpu.VMEM((IDX_DIM, IDX_HEADS * tq), BF16), pltpu.VMEM((DSA_LATENT, DSA_HEADS * tq), BF16),
                        pltpu.VMEM((seq, tq), F32), pltpu.VMEM((seq, tq), BF16), pltpu.VMEM((seq, tq), F32),
                        pltpu.VMEM((seq, DSA_HEADS * tq), F32), pltpu.VMEM((DSA_HEADS, CKVT_ROWS, tq), F32),
                        pltpu.VMEM((8, tq), I32), pltpu.VMEM((DSA_WIDTH, tq), F32),
                        pltpu.VMEM((8, DSA_HEADS * tq), F32)],
        compiler_params=_params("parallel", "arbitrary"),
        name="dsa",
    )(p["iqt"], p["wit"], p["dqt"], p["ki"], p["ckv"], p["ckvt"], wuv_t)


def _memkv_kernel(m_ref, g_ref, w_ref, o_ref):
    o_ref[...] = _dot(_rmsnorm(m_ref[...], g_ref[...]).astype(BF16), w_ref[...].astype(BF16)).astype(BF16)


def _memkv(mem2d, g, w_kv, mem_len):
    rows, d = mem2d.shape
    return pl.pallas_call(
        _memkv_kernel,
        grid=(rows // mem_len,),
        in_specs=[pl.BlockSpec((mem_len, d), lambda b: (b, 0)), _full((1, d)), _resident(w_kv.shape)],
        out_specs=pl.BlockSpec((mem_len, w_kv.shape[1]), lambda b: (b, 0)),
        out_shape=jax.ShapeDtypeStruct((rows, w_kv.shape[1]), BF16),
        compiler_params=_params("parallel"),
        name="memkv",
    )(mem2d, g, w_kv)


def _mixout_kernel(h_ref, yml_ref, ydsa_ref, woa_ref, wob_ref, g_ref, wq_ref, kv_ref, wo_ref, o_ref):
    d = h_ref.shape[1]
    hd = d // XA_HEADS
    h2 = h_ref[...] + _dot(yml_ref[...], woa_ref[...].astype(BF16)) + _dot(ydsa_ref[...], wob_ref[...].astype(BF16))
    q = _dot(_rmsnorm(h2, g_ref[...]).astype(BF16), wq_ref[...].astype(BF16)).astype(BF16)
    heads = []
    for h in range(XA_HEADS):
        kh = kv_ref[:, h * hd:(h + 1) * hd]
        vh = kv_ref[:, d + h * hd:d + (h + 1) * hd]
        lg = _dot_nt(q[:, h * hd:(h + 1) * hd], kh) * (hd ** -0.5)
        p = jnp.exp(lg - jnp.max(lg, axis=1, keepdims=True))
        denom = jnp.sum(p, axis=1, keepdims=True)
        heads.append((_dot(p.astype(BF16), vh) / denom).astype(BF16))
    o_ref[...] = h2 + _dot(jnp.concatenate(heads, axis=1), wo_ref[...].astype(BF16))


def _mixout(h1, yml, ydsa, w, kv, batch, seq, mem_len):
    t, d = h1.shape
    tm = MIX_TM
    nt = seq // tm
    row = lambda n: pl.BlockSpec((tm, n), lambda b, j: (b * nt + j, 0))
    return pl.pallas_call(
        _mixout_kernel,
        grid=(batch, nt),
        in_specs=[row(d), row(ML_WIDTH), row(DSA_WIDTH), _resident(w["woa"].shape), _resident(w["wob"].shape),
                  _full((1, d)), _resident(w["wq"].shape),
                  pl.BlockSpec((mem_len, 2 * d), lambda b, j: (b, 0)), _resident(w["wxo"].shape)],
        out_specs=row(d),
        out_shape=jax.ShapeDtypeStruct((t, d), F32),
        compiler_params=_params("parallel", "parallel"),
        name="mixout",
    )(h1, yml, ydsa, w["woa"], w["wob"], w["xg"], w["wq"], kv, w["wxo"])


def _layer(h, mem2d, p, batch, seq, mem_len, final_g):
    h1 = _ffn(h, p["f1g"], p["f1wg"], p["f1wu"], p["f1wd"], p["f1g"], final_norm=False)
    pr = _proj(h1, p["mixg"], p, seq)
    yml = _mlstm(pr, p["headg"], batch, seq)
    ydsa = _dsa(pr, p["wuv_t"], batch, seq)
    kv = _memkv(mem2d, p["memg"], p["wkv"], mem_len)
    h3 = _mixout(h1, yml, ydsa, p, kv, batch, seq, mem_len)
    fg = p["f2g"] if final_g is None else final_g
    return _ffn(h3, p["f2g"], p["f2wg"], p["f2wu"], p["f2wd"], fg, final_norm=final_g is not None)


def kernel(x, mem, ffn1_norm_g, ffn1_w_gate, ffn1_w_up, ffn1_w_down, mix_norm_g, w_in, mlstm_conv_w, mlstm_conv_b, mlstm_i_bias, mlstm_f_bias, mlstm_head_norm_g, dsa_kv_norm_g, idx_k_norm_g, dsa_w_uv, w_out, xattn_norm_g, mem_norm_g, xattn_w_q, xattn_w_kv, xattn_w_o, ffn2_norm_g, ffn2_w_gate, ffn2_w_up, ffn2_w_down, final_norm_g):
    batch, seq, d = x.shape
    mem_len = mem.shape[1]
    depth = w_in.shape[0]
    h = x.reshape(batch * seq, d)
    mem2d = mem.reshape(batch * mem_len, d)
    row = lambda a: a.reshape(1, -1).astype(F32)
    b16 = lambda a: a.astype(BF16)

    splits = (ML_WIDTH, ML_WIDTH, ML_WIDTH, ML_HEADS, ML_HEADS, ML_WIDTH, DSA_HEADS * DSA_LATENT, DSA_LATENT,
              IDX_HEADS * IDX_DIM, IDX_DIM, IDX_HEADS)
    offs = [0]
    for s in splits:
        offs.append(offs[-1] + s)

    for l in range(depth):
        wi = w_in[l]
        cols = [wi[:, offs[i]:offs[i + 1]] for i in range(len(splits))]
        mq, mk, mv, mi, mf, mo, dq, dc, iq, ik, iw = cols
        w_gate = jnp.concatenate([mi, mf], axis=1)
        b_gate = jnp.concatenate([mlstm_i_bias[l], mlstm_f_bias[l]])
        p = {
            "f1g": row(ffn1_norm_g[l]), "f1wg": ffn1_w_gate[l], "f1wu": ffn1_w_up[l], "f1wd": ffn1_w_down[l],
            "f2g": row(ffn2_norm_g[l]), "f2wg": ffn2_w_gate[l], "f2wu": ffn2_w_up[l], "f2wd": ffn2_w_down[l],
            "mixg": row(mix_norm_g[l]),
            "wqk": b16(jnp.concatenate([mq, mk], axis=1)), "wv": b16(mv), "wo": b16(mo),
            "wgc": b16(jnp.pad(w_gate, ((0, 0), (0, 128 - 2 * ML_HEADS)))),
            "wgr": b16(jnp.pad(w_gate.T, ((0, 16 - 2 * ML_HEADS), (0, 0)))),
            "bgc": jnp.pad(b_gate, (0, 128 - 2 * ML_HEADS)).reshape(1, 128).astype(F32),
            "bgr": jnp.pad(b_gate, (0, 16 - 2 * ML_HEADS)).reshape(16, 1).astype(F32),
            "cw": mlstm_conv_w[l].astype(F32), "cb": row(mlstm_conv_b[l]),
            "wdq_t": b16(dq.T), "wiq_t": b16(iq.T), "wiw_t": b16(jnp.pad(iw.T, ((0, 16 - IDX_HEADS), (0, 0)))),
            "wdc": b16(dc), "kvg": row(dsa_kv_norm_g[l]),
            "wik": b16(jnp.pad(ik, ((0, 0), (0, 128 - IDX_DIM)))), "ikg": row(idx_k_norm_g[l]),
            "headg": row(mlstm_head_norm_g[l]),
            "wuv_t": b16(jnp.swapaxes(dsa_w_uv[l], 1, 2)),
            "woa": w_out[l][:ML_WIDTH], "wob": w_out[l][ML_WIDTH:],
            "xg": row(xattn_norm_g[l]), "memg": row(mem_norm_g[l]),
            "wq": xattn_w_q[l], "wkv": xattn_w_kv[l], "wxo": xattn_w_o[l],
        }
        fg = row(final_norm_g) if l == depth - 1 else None
        h = _layer(h, mem2d, p, batch, seq, mem_len, fg)
    return h.reshape(batch, seq, d)
```

```python
import functools

import jax
import jax.numpy as jnp
from jax import lax
from jax.experimental import pallas as pl
from jax.experimental.pallas import tpu as pltpu

F32 = jnp.float32
BF16 = jnp.bfloat16
I32 = jnp.int32
EPS = 1e-6

ML_HEADS = 4
ML_HEAD_DIM = 128
ML_WIDTH = ML_HEADS * ML_HEAD_DIM
ML_CONV = 4
DSA_HEADS = 8
DSA_LATENT = 128
DSA_HEAD_DIM = 64
DSA_WIDTH = DSA_HEADS * DSA_HEAD_DIM
IDX_HEADS = 8
IDX_DIM = 64
INDEX_TOPK = 256
XA_HEADS = 4

FFN_TM = 512
FFN_FC = 256
PROJ_TM = 1024
ML_CHUNK = 128
ML_SEQS = 2
DSA_TQ = 256
MIX_TM = 1024
MEMKV_TM = 1024

CKVT_ROWS = DSA_LATENT + 16

VMEM_LIMIT = 56 * 1024 * 1024
INT_MIN = -(2 ** 31)
NEG_INF = float("-inf")
CODE_NEG_INF = 0x007FFFFF
LOG2E = 1.4426950408889634


def _rmsnorm(x, g):
    return x * lax.rsqrt(jnp.mean(x * x, axis=-1, keepdims=True) + EPS) * g


def _sigmoid(x):
    return 1.0 / (1.0 + jnp.exp(-x))


def _dot(a, b):
    return jnp.dot(a, b, preferred_element_type=F32)


def _dot_nt(a, b):
    return lax.dot_general(a, b, (((1,), (1,)), ((), ())), preferred_element_type=F32)


def _full(shape):
    return pl.BlockSpec(shape, lambda *_: (0,) * len(shape))


def _resident(shape):
    return pl.BlockSpec(shape, lambda *_: (0,) * len(shape), pipeline_mode=pl.Buffered(1))


def _params(*sem):
    return pltpu.CompilerParams(dimension_semantics=sem, vmem_limit_bytes=VMEM_LIMIT)


def _ffn_kernel(x_ref, g_ref, wg_ref, wu_ref, wd_ref, fg_ref, o_ref, *, final_norm):
    x = x_ref[...]
    xn = _rmsnorm(x, g_ref[...]).astype(BF16)
    d_ff = wg_ref.shape[1]
    acc = jnp.zeros(x.shape, F32)
    for c in range(d_ff // FFN_FC):
        sl = slice(c * FFN_FC, (c + 1) * FFN_FC)
        gate = _dot(xn, wg_ref[:, sl].astype(BF16))
        up = _dot(xn, wu_ref[:, sl].astype(BF16))
        act = (gate * _sigmoid(gate) * up).astype(BF16)
        acc = acc + _dot(act, wd_ref[sl, :].astype(BF16))
    y = x + 0.5 * acc
    if final_norm:
        y = _rmsnorm(y, fg_ref[...])
    o_ref[...] = y


def _ffn(x, g, wg, wu, wd, fg, final_norm):
    t, d = x.shape
    d_ff = wg.shape[1]
    return pl.pallas_call(
        functools.partial(_ffn_kernel, final_norm=final_norm),
        grid=(t // FFN_TM,),
        in_specs=[
            pl.BlockSpec((FFN_TM, d), lambda i: (i, 0)),
            _full((1, d)),
            _resident((d, d_ff)),
            _resident((d, d_ff)),
            _resident((d_ff, d)),
            _full((1, d)),
        ],
        out_specs=pl.BlockSpec((FFN_TM, d), lambda i: (i, 0)),
        out_shape=jax.ShapeDtypeStruct((t, d), F32),
        compiler_params=_params("parallel"),
        name="ffn_final" if final_norm else "ffn",
    )(x, g, wg, wu, wd, fg)


def _proj_kernel(h_ref, g_ref, wqk_ref, wv_ref, wo_ref, wgc_ref, wgr_ref, bgc_ref, bgr_ref,
                 cw_ref, cb_ref, wdq_ref, wiq_ref, wiw_ref, wdc_ref, kvg_ref, wik_ref, ikg_ref,
                 q_ref, k_ref, v_ref, o_ref, gc_ref, gr_ref, dqt_ref, iqt_ref, wit_ref,
                 ckv_ref, ckvt_ref, ki_ref, zbuf, *, tiles_per_seq):
    tm = h_ref.shape[0]
    u = _rmsnorm(h_ref[...], g_ref[...]).astype(BF16)

    @pl.when(pl.program_id(0) % tiles_per_seq == 0)
    def _():
        zbuf[0:8, :] = jnp.zeros((8, zbuf.shape[1]), F32)

    zbuf[8:8 + tm, :] = _dot(u, wqk_ref[...])
    y = jnp.zeros((tm, zbuf.shape[1]), F32) + cb_ref[...]
    for j in range(ML_CONV):
        y = y + zbuf[5 + j:5 + j + tm, :] * cw_ref[j:j + 1, :]
    zbuf[0:8, :] = zbuf[tm:tm + 8, :]
    y = y * _sigmoid(y)
    q_ref[...] = y[:, :ML_WIDTH].astype(BF16)
    k_ref[...] = (y[:, ML_WIDTH:] * (ML_HEAD_DIM ** -0.5)).astype(BF16)
    v_ref[...] = _dot(u, wv_ref[...]).astype(BF16)
    o_ref[...] = _dot(u, wo_ref[...]).astype(BF16)

    def gates(z, idx):
        return jnp.where(idx < ML_HEADS, z, jnp.minimum(z, 0.0) - jnp.log(1.0 + jnp.exp(-jnp.abs(z))))

    zc = _dot(u, wgc_ref[...]) + bgc_ref[...]
    gc_ref[...] = gates(zc, lax.broadcasted_iota(I32, zc.shape, 1))
    zr = _dot_nt(wgr_ref[...], u) + bgr_ref[...]
    gr_ref[...] = gates(zr, lax.broadcasted_iota(I32, zr.shape, 0))

    dqt_ref[...] = (_dot_nt(wdq_ref[...], u) * (DSA_LATENT ** -0.5 * LOG2E)).astype(BF16)
    iqt_ref[...] = _dot_nt(wiq_ref[...], u).astype(BF16)
    wit_ref[...] = _dot_nt(wiw_ref[...], u) * (IDX_HEADS ** -0.5) * (IDX_DIM ** -0.5)
    ckv = _rmsnorm(_dot(u, wdc_ref[...]), kvg_ref[...])
    ckv_ref[...] = ckv.astype(BF16)
    ckvt_ref[0:DSA_LATENT, :] = ckv.T.astype(BF16)
    ckvt_ref[DSA_LATENT:, :] = (lax.broadcasted_iota(I32, (CKVT_ROWS - DSA_LATENT, tm), 0) == 0).astype(BF16)
    ik = _dot(u, wik_ref[...])[:, :IDX_DIM]
    ki_ref[...] = _rmsnorm(ik, ikg_ref[...]).astype(BF16)


def _proj(h1, g, w, seq):
    t, d = h1.shape
    tm = PROJ_TM
    row = lambda n: pl.BlockSpec((tm, n), lambda i: (i, 0))
    col = lambda n: pl.BlockSpec((n, tm), lambda i: (0, i))
    outs = [
        ("q", row(ML_WIDTH), (t, ML_WIDTH), BF16),
        ("k", row(ML_WIDTH), (t, ML_WIDTH), BF16),
        ("v", row(ML_WIDTH), (t, ML_WIDTH), BF16),
        ("o", row(ML_WIDTH), (t, ML_WIDTH), BF16),
        ("gc", row(128), (t, 128), F32),
        ("gr", col(16), (16, t), F32),
        ("dqt", col(DSA_HEADS * DSA_LATENT), (DSA_HEADS * DSA_LATENT, t), BF16),
        ("iqt", col(IDX_HEADS * IDX_DIM), (IDX_HEADS * IDX_DIM, t), BF16),
        ("wit", col(16), (16, t), F32),
        ("ckv", row(DSA_LATENT), (t, DSA_LATENT), BF16),
        ("ckvt", col(CKVT_ROWS), (CKVT_ROWS, t), BF16),
        ("ki", row(IDX_DIM), (t, IDX_DIM), BF16),
    ]
    ins = [h1, g, w["wqk"], w["wv"], w["wo"], w["wgc"], w["wgr"], w["bgc"], w["bgr"], w["cw"], w["cb"],
           w["wdq_t"], w["wiq_t"], w["wiw_t"], w["wdc"], w["kvg"], w["wik"], w["ikg"]]
    res = pl.pallas_call(
        functools.partial(_proj_kernel, tiles_per_seq=seq // tm),
        grid=(t // tm,),
        in_specs=[row(d)] + [_resident(a.shape) for a in ins[1:]],
        out_specs=[o[1] for o in outs],
        out_shape=[jax.ShapeDtypeStruct(o[2], o[3]) for o in outs],
        scratch_shapes=[pltpu.VMEM((tm + 8, 2 * ML_WIDTH), F32)],
        compiler_params=_params("arbitrary"),
        name="proj",
    )(*ins)
    return {o[0]: r for o, r in zip(outs, res)}


def _split3(x):
    hi = x.astype(BF16)
    r1 = x - hi.astype(F32)
    mid = r1.astype(BF16)
    lo = (r1 - mid.astype(F32)).astype(BF16)
    return hi, mid, lo


def _mlstm_kernel(q_ref, k_ref, v_ref, o_ref, gc_ref, gr_ref, hg_ref, y_ref, c_ref, m_ref, *, seq):
    L = ML_CHUNK
    d = ML_HEAD_DIM
    c_ref[...] = jnp.zeros(c_ref.shape, F32)
    m_ref[...] = jnp.zeros(m_ref.shape, F32)

    ri = lax.broadcasted_iota(I32, (L, L), 0)
    ci = lax.broadcasted_iota(I32, (L, L), 1)
    causal = ci <= ri
    tri_lo = causal.astype(BF16)
    tri_up = (ri <= ci).astype(BF16)
    ones_col = (lax.broadcasted_iota(I32, (L, d), 1) == 0).astype(F32)

    nch = ML_SEQS * ML_HEADS
    causal_all = jnp.concatenate([causal] * nch, axis=0)
    stack = lambda parts: jnp.concatenate(parts, axis=0)
    col_of = lambda x4: stack([jnp.broadcast_to(x4[j:j + 1, :], (L, 1)) for j in range(nch)])
    gain = stack([jnp.broadcast_to(hg_ref[:, (j % ML_HEADS) * d:(j % ML_HEADS + 1) * d], (L, d)) for j in range(nch)])

    def chunk(c, carry):
        r0s = [pl.multiple_of(sq * seq + c * L, L) for sq in range(ML_SEQS)]
        hsl = lambda j: slice((j % ML_HEADS) * d, (j % ML_HEADS + 1) * d)
        rws = lambda j: pl.ds(r0s[j // ML_HEADS], L)
        bcs, brs, gcs, grs = [], [], [], []
        for sq in range(ML_SEQS):
            gc = gc_ref[pl.ds(r0s[sq], L), :]
            gr = gr_ref[:, pl.ds(r0s[sq], L)]
            bcs.append(sum(_dot(tri_lo, p) for p in _split3(gc)))
            brs.append(sum(_dot(p, tri_up) for p in _split3(gr)))
            gcs.append(gc)
            grs.append(gr)
        b_col = stack([bcs[j // ML_HEADS][:, ML_HEADS + j % ML_HEADS:ML_HEADS + j % ML_HEADS + 1] for j in range(nch)])
        i_col = stack([gcs[j // ML_HEADS][:, j % ML_HEADS:j % ML_HEADS + 1] for j in range(nch)])
        b_row4 = stack([brs[sq][ML_HEADS:2 * ML_HEADS, :] for sq in range(ML_SEQS)])
        i_row4 = stack([grs[sq][0:ML_HEADS, :] for sq in range(ML_SEQS)])
        u_row4 = i_row4 - b_row4
        u_blk = stack([jnp.broadcast_to(u_row4[j:j + 1, :], (L, L)) for j in range(nch)])
        m_prev4 = m_ref[0:nch, 0:1]
        logw = jnp.where(causal_all, b_col + u_blk, NEG_INF)
        inter = b_col + col_of(m_prev4)
        mj = jnp.maximum(inter, jnp.max(logw, axis=1, keepdims=True))
        w = jnp.exp(logw - mj)
        a = jnp.exp(inter - mj)
        qs = [q_ref[rws(j), hsl(j)] for j in range(nch)]
        ks = [k_ref[rws(j), hsl(j)] for j in range(nch)]
        vaugs = [jnp.concatenate([v_ref[rws(j), hsl(j)].astype(F32), ones_col], axis=1) for j in range(nch)]
        sqk = (stack([_dot_nt(qs[j], ks[j]) for j in range(nch)]) * w).astype(BF16)
        cts = [c_ref[j] for j in range(nch)]
        tot = a * stack([_dot(qs[j], cts[j].astype(BF16)) for j in range(nch)]) + stack(
            [_dot(sqk[j * L:(j + 1) * L, :], vaugs[j].astype(BF16)) for j in range(nch)])
        hh = tot[:, :d] / jnp.maximum(jnp.abs(tot[:, d:d + 1]), jnp.exp(-mj))
        gate = _sigmoid(stack([o_ref[rws(j), hsl(j)] for j in range(nch)]).astype(F32))
        y = (_rmsnorm(hh, gain) * gate).astype(BF16)
        for j in range(nch):
            y_ref[rws(j), hsl(j)] = y[j * L:(j + 1) * L, :]
        b_last4 = b_row4[:, L - 1:L]
        g_row4 = b_last4 + u_row4
        m_new4 = jnp.maximum(b_last4 + m_prev4, jnp.max(g_row4, axis=1, keepdims=True))
        dec4 = jnp.exp(b_last4 + m_prev4 - m_new4)
        wg = jnp.exp(col_of(b_last4 - m_new4) - b_col + i_col)
        for j in range(nch):
            kt = ks[j].astype(F32).T.astype(BF16)
            wv = (wg[j * L:(j + 1) * L, :] * vaugs[j]).astype(BF16)
            c_ref[j] = dec4[j:j + 1, :] * cts[j] + _dot(kt, wv)
        m_ref[0:nch, :] = jnp.broadcast_to(m_new4, (nch, m_ref.shape[1]))
        return carry

    lax.fori_loop(0, seq // L, chunk, 0)


def _mlstm(p, head_g, batch, seq):
    t = batch * seq
    rows = ML_SEQS * seq
    row = lambda n: pl.BlockSpec((rows, n), lambda b: (b, 0))
    return pl.pallas_call(
        functools.partial(_mlstm_kernel, seq=seq),
        grid=(batch // ML_SEQS,),
        in_specs=[row(ML_WIDTH), row(ML_WIDTH), row(ML_WIDTH), row(ML_WIDTH), row(128),
                  pl.BlockSpec((16, rows), lambda b: (0, b)), _full((1, ML_WIDTH))],
        out_specs=row(ML_WIDTH),
        out_shape=jax.ShapeDtypeStruct((t, ML_WIDTH), BF16),
        scratch_shapes=[pltpu.VMEM((ML_SEQS * ML_HEADS, ML_HEAD_DIM, 2 * ML_HEAD_DIM), F32),
                        pltpu.VMEM((ML_SEQS * ML_HEADS, 128), F32)],
        compiler_params=_params("parallel"),
        name="mlstm",
    )(p["q"], p["k"], p["v"], p["o"], p["gc"], p["gr"], head_g)


def _float_of_code(code):
    key = code ^ jnp.int32(INT_MIN)
    bits = jnp.where(key < 0, key ^ jnp.int32(0x7FFFFFFF), key)
    return lax.bitcast_convert_type(bits, F32)


def _colmax8(x):
    parts = [x[8 * j:8 * j + 8, :] for j in range(x.shape[0] // 8)]
    while len(parts) > 1:
        parts = [jnp.maximum(parts[j], parts[j + 1]) for j in range(0, len(parts) - 1, 2)] + (
            [parts[-1]] if len(parts) % 2 else [])
    return parts[0]


def _colsum(x):
    groups = x.shape[0] // 8
    lanes = min(2, groups)
    parts = [x[8 * j:8 * j + 8, :] for j in range(lanes)]
    for j in range(lanes, groups):
        parts[j % lanes] = parts[j % lanes] + x[8 * j:8 * j + 8, :]
    while len(parts) > 1:
        parts = [parts[j] + parts[j + 1] for j in range(0, len(parts), 2)]
    return jnp.sum(parts[0], axis=0, keepdims=True)


def _colsum_bf16(x):
    groups = x.shape[0] // 16
    assert x.shape[0] % 16 == 0 and groups <= 2 * 256
    lanes = min(2, groups)
    parts = [x[16 * j:16 * j + 16, :] for j in range(lanes)]
    for j in range(lanes, groups):
        parts[j % lanes] = parts[j % lanes] + x[16 * j:16 * j + 16, :]
    tot = parts[0].astype(F32)
    for part in parts[1:]:
        tot = tot + part.astype(F32)
    return jnp.sum(tot, axis=0, keepdims=True)


def _dsa_kernel(iqt_ref, wit_ref, dqt_ref, ki_ref, ckv_ref, ckvt_ref, wuvt_ref, y_ref,
                iqa_ref, dqa_ref, score_ref, sb_ref, bias_ref, lg_ref, acc_ref, code_ref, yt_ref, m8_ref, *, topk, nq):
    tq = iqt_ref.shape[1]
    nh = DSA_HEADS
    nk = pl.program_id(1) + 1
    last = nk - 1
    rows = lambda c: pl.ds(pl.multiple_of(c * tq, tq), tq)
    hs = lambda h: slice(h * tq, (h + 1) * tq)
    dmat = lax.broadcasted_iota(I32, (tq, tq), 0) - lax.broadcasted_iota(I32, (tq, tq), 1)
    allowed = lambda c: dmat <= jnp.where(c < last, tq, 0)

    for h in range(nh):
        iqa_ref[:, hs(h)] = iqt_ref[h * IDX_DIM:(h + 1) * IDX_DIM, :]
        dqa_ref[:, hs(h)] = dqt_ref[h * DSA_LATENT:(h + 1) * DSA_LATENT, :]
    wi_all = jnp.concatenate([wit_ref[h:h + 1, :] for h in range(IDX_HEADS)], axis=1)

    def for_blocks(body):
        def quad(p, carry):
            for u in range(4):
                body(4 * p + u)
            return carry

        def pair():
            body(nk & ~3)
            body((nk & ~3) + 1)

        lax.fori_loop(0, lax.shift_right_logical(nk, 2), quad, 0)
        pl.when((nk & 2) != 0)(pair)
        pl.when((nk & 1) != 0)(lambda: body(last))

    def idx_block(c):
        w = wi_all * jnp.maximum(_dot(ki_ref[rows(c), :], iqa_ref[...]), 0.0)
        acc = w[:, hs(0)]
        for h in range(1, IDX_HEADS):
            acc = acc + w[:, hs(h)]
        sc = jnp.where(allowed(c), acc, NEG_INF)
        score_ref[rows(c), :] = sc
        sb_ref[rows(c), :] = sc.astype(BF16)

    for_blocks(idx_block)

    def search(n):
        if n * tq <= topk:
            code_ref[0:1, :] = jnp.full((1, tq), CODE_NEG_INF, I32)
            return

        def hi_pass(i, code):
            cand_code = code | lax.shift_left(jnp.int32(1), 15 - i)
            cand = _float_of_code(lax.shift_left(cand_code, 16)).astype(BF16)
            ind = jnp.where(sb_ref[0:n * tq, :] < cand, jnp.zeros((), BF16), jnp.ones((), BF16))
            return jnp.where(_colsum_bf16(ind) >= topk, cand_code, code)

        c1 = lax.fori_loop(0, 16, hi_pass, jnp.zeros((1, tq), I32))
        base = lax.shift_left(c1, 16) - jnp.int32(0x8001)

        def lo_pass(i, off):
            cand_off = off | lax.shift_left(jnp.int32(1), 16 - i)
            cand = _float_of_code(base + cand_off)
            cnt = _colsum(jnp.where(score_ref[0:n * tq, :] < cand, 0.0, 1.0))
            return jnp.where(cnt >= topk, cand_off, off)

        code_ref[0:1, :] = base + lax.fori_loop(0, 17, lo_pass, jnp.zeros((1, tq), I32))

    for n in range(1, nq + 1):
        pl.when(nk == n)(functools.partial(search, n))
    thr = _float_of_code(code_ref[0:1, :])

    def mask_block(c, carry):
        n_gt, n_eq = carry
        s = score_ref[rows(c), :]
        bias_ref[rows(c), :] = jnp.where((s >= thr) & allowed(c), 0.0, NEG_INF)
        n_gt = n_gt + _colsum(jnp.where(s > thr, 1.0, 0.0))
        n_eq = n_eq + _colsum(jnp.where(s == thr, 1.0, 0.0))
        return n_gt, n_eq

    zero = jnp.zeros((1, tq), F32)
    n_gt, n_eq = lax.fori_loop(0, nk, mask_block, (zero, zero))
    room = topk - n_gt

    @pl.when(jnp.max(n_eq - room) > 0.0)
    def _():
        lo = (dmat > 0).astype(BF16)

        def tie_block(c, before):
            s = score_ref[rows(c), :]
            ef = jnp.where(s == thr, 1.0, 0.0)
            rank = before + _dot(lo, ef.astype(BF16))
            keep = ((s > thr) | ((s == thr) & (rank < room))) & allowed(c)
            bias_ref[rows(c), :] = jnp.where(keep, 0.0, NEG_INF)
            return before + jnp.sum(ef, axis=0, keepdims=True)

        lax.fori_loop(0, nk, tie_block, zero)

    m8_ref[...] = jnp.full(m8_ref.shape, NEG_INF, F32)

    def logit_block(c):
        s_all = _dot(ckv_ref[rows(c), :], dqa_ref[...])
        b = bias_ref[rows(c), :]
        for h in range(nh):
            s = s_all[:, hs(h)] + b
            lg_ref[rows(c), hs(h)] = s
            m8_ref[:, hs(h)] = jnp.maximum(m8_ref[:, hs(h)], _colmax8(s))

    for_blocks(logit_block)
    m = jnp.max(m8_ref[...], axis=0, keepdims=True)
    acc_ref[...] = jnp.zeros(acc_ref.shape, F32)

    def prob_block(c):
        ckvt = ckvt_ref[:, rows(c)]
        for h in range(nh):
            p = jnp.exp2(lg_ref[rows(c), hs(h)] - m[:, hs(h)]).astype(BF16)
            acc_ref[h] += _dot(ckvt, p)

    for_blocks(prob_block)
    for h in range(nh):
        ot = acc_ref[h, 0:DSA_LATENT, :] / acc_ref[h, DSA_LATENT:DSA_LATENT + 1, :]
        yt_ref[h * DSA_HEAD_DIM:(h + 1) * DSA_HEAD_DIM, :] = _dot(wuvt_ref[h], ot.astype(BF16))
    y_ref[...] = yt_ref[...].T.astype(BF16)


def _dsa(p, wuv_t, batch, seq):
    t = batch * seq
    tq = DSA_TQ
    nq = seq // tq
    topk = min(INDEX_TOPK, seq // 4)
    qcol = lambda n: pl.BlockSpec((n, tq), lambda b, i: (0, b * nq + i))
    return pl.pallas_call(
        functools.partial(_dsa_kernel, topk=float(topk), nq=nq),
        grid=(batch, nq),
        in_specs=[qcol(IDX_HEADS * IDX_DIM), qcol(16), qcol(DSA_HEADS * DSA_LATENT),
                  pl.BlockSpec((seq, IDX_DIM), lambda b, i: (b, 0)),
                  pl.BlockSpec((seq, DSA_LATENT), lambda b, i: (b, 0)),
                  pl.BlockSpec((CKVT_ROWS, seq), lambda b, i: (0, b)),
                  _full(wuv_t.shape)],
        out_specs=pl.BlockSpec((tq, DSA_WIDTH), lambda b, i: (b * nq + i, 0)),
        out_shape=jax.ShapeDtypeStruct((t, DSA_WIDTH), BF16),
        scratch_shapes=[pltpu.VMEM((IDX_DIM, IDX_HEADS * tq), BF16), pltpu.VMEM((DSA_LATENT, DSA_HEADS * tq), BF16),
                        pltpu.VMEM((seq, tq), F32), pltpu.VMEM((seq, tq), BF16), pltpu.VMEM((seq, tq), F32),
                        pltpu.VMEM((seq, DSA_HEADS * tq), F32), pltpu.VMEM((DSA_HEADS, CKVT_ROWS, tq), F32),
                        pltpu.VMEM((8, tq), I32), pltpu.VMEM((DSA_WIDTH, tq), F32),
                        pltpu.VMEM((8, DSA_HEADS * tq), F32)],
        compiler_params=_params("parallel", "arbitrary"),
        name="dsa",
    )(p["iqt"], p["wit"], p["dqt"], p["ki"], p["ckv"], p["ckvt"], wuv_t)


def _memkv_kernel(m_ref, g_ref, w_ref, o_ref):
    o_ref[...] = _dot(_rmsnorm(m_ref[...], g_ref[...]).astype(BF16), w_ref[...].astype(BF16)).astype(BF16)


def _memkv(mem2d, g, w_kv, mem_len):
    rows, d = mem2d.shape
    tm = MEMKV_TM if rows % MEMKV_TM == 0 else mem_len
    return pl.pallas_call(
        _memkv_kernel,
        grid=(rows // tm,),
        in_specs=[pl.BlockSpec((tm, d), lambda b: (b, 0)), _full((1, d)), _resident(w_kv.shape)],
        out_specs=pl.BlockSpec((tm, w_kv.shape[1]), lambda b: (b, 0)),
        out_shape=jax.ShapeDtypeStruct((rows, w_kv.shape[1]), BF16),
        compiler_params=_params("parallel"),
        name="memkv",
    )(mem2d, g, w_kv)


def _mixout_kernel(h_ref, yml_ref, ydsa_ref, woa_ref, wob_ref, g_ref, wq_ref, kv_ref, wo_ref, o_ref):
    d = h_ref.shape[1]
    hd = d // XA_HEADS
    h2 = h_ref[...] + _dot(yml_ref[...], woa_ref[...].astype(BF16)) + _dot(ydsa_ref[...], wob_ref[...].astype(BF16))
    q = _dot(_rmsnorm(h2, g_ref[...]).astype(BF16), wq_ref[...].astype(BF16)).astype(BF16)
    heads = []
    for h in range(XA_HEADS):
        kh = kv_ref[:, h * hd:(h + 1) * hd]
        vh = kv_ref[:, d + h * hd:d + (h + 1) * hd]
        lg = _dot_nt(q[:, h * hd:(h + 1) * hd], kh) * (hd ** -0.5)
        p = jnp.exp(lg - jnp.max(lg, axis=1, keepdims=True))
        denom = jnp.sum(p, axis=1, keepdims=True)
        heads.append((_dot(p.astype(BF16), vh) / denom).astype(BF16))
    o_ref[...] = h2 + _dot(jnp.concatenate(heads, axis=1), wo_ref[...].astype(BF16))


def _mixout(h1, yml, ydsa, w, kv, batch, seq, mem_len):
    t, d = h1.shape
    tm = MIX_TM
    nt = seq // tm
    row = lambda n: pl.BlockSpec((tm, n), lambda b, j: (b * nt + j, 0))
    return pl.pallas_call(
        _mixout_kernel,
        grid=(batch, nt),
        in_specs=[row(d), row(ML_WIDTH), row(DSA_WIDTH), _resident(w["woa"].shape), _resident(w["wob"].shape),
                  _full((1, d)), _resident(w["wq"].shape),
                  pl.BlockSpec((mem_len, 2 * d), lambda b, j: (b, 0)), _resident(w["wxo"].shape)],
        out_specs=row(d),
        out_shape=jax.ShapeDtypeStruct((t, d), F32),
        compiler_params=_params("parallel", "parallel"),
        name="mixout",
    )(h1, yml, ydsa, w["woa"], w["wob"], w["xg"], w["wq"], kv, w["wxo"])


def _layer(h, mem2d, p, batch, seq, mem_len, final_g):
    h1 = _ffn(h, p["f1g"], p["f1wg"], p["f1wu"], p["f1wd"], p["f1g"], final_norm=False)
    pr = _proj(h1, p["mixg"], p, seq)
    yml = _mlstm(pr, p["headg"], batch, seq)
    ydsa = _dsa(pr, p["wuv_t"], batch, seq)
    kv = _memkv(mem2d, p["memg"], p["wkv"], mem_len)
    h3 = _mixout(h1, yml, ydsa, p, kv, batch, seq, mem_len)
    fg = p["f2g"] if final_g is None else final_g
    return _ffn(h3, p["f2g"], p["f2wg"], p["f2wu"], p["f2wd"], fg, final_norm=final_g is not None)


def kernel(x, mem, ffn1_norm_g, ffn1_w_gate, ffn1_w_up, ffn1_w_down, mix_norm_g, w_in, mlstm_conv_w, mlstm_conv_b, mlstm_i_bias, mlstm_f_bias, mlstm_head_norm_g, dsa_kv_norm_g, idx_k_norm_g, dsa_w_uv, w_out, xattn_norm_g, mem_norm_g, xattn_w_q, xattn_w_kv, xattn_w_o, ffn2_norm_g, ffn2_w_gate, ffn2_w_up, ffn2_w_down, final_norm_g):
    batch, seq, d = x.shape
    mem_len = mem.shape[1]
    depth = w_in.shape[0]
    h = x.reshape(batch * seq, d)
    mem2d = mem.reshape(batch * mem_len, d)
    row = lambda a: a.reshape(1, -1).astype(F32)
    b16 = lambda a: a.astype(BF16)

    splits = (ML_WIDTH, ML_WIDTH, ML_WIDTH, ML_HEADS, ML_HEADS, ML_WIDTH, DSA_HEADS * DSA_LATENT, DSA_LATENT,
              IDX_HEADS * IDX_DIM, IDX_DIM, IDX_HEADS)
    offs = [0]
    for s in splits:
        offs.append(offs[-1] + s)

    for l in range(depth):
        wi = w_in[l]
        cols = [wi[:, offs[i]:offs[i + 1]] for i in range(len(splits))]
        mq, mk, mv, mi, mf, mo, dq, dc, iq, ik, iw = cols
        w_gate = jnp.concatenate([mi, mf], axis=1)
        b_gate = jnp.concatenate([mlstm_i_bias[l], mlstm_f_bias[l]])
        p = {
            "f1g": row(ffn1_norm_g[l]), "f1wg": ffn1_w_gate[l], "f1wu": ffn1_w_up[l], "f1wd": ffn1_w_down[l],
            "f2g": row(ffn2_norm_g[l]), "f2wg": ffn2_w_gate[l], "f2wu": ffn2_w_up[l], "f2wd": ffn2_w_down[l],
            "mixg": row(mix_norm_g[l]),
            "wqk": b16(jnp.concatenate([mq, mk], axis=1)), "wv": b16(mv), "wo": b16(mo),
            "wgc": b16(jnp.pad(w_gate, ((0, 0), (0, 128 - 2 * ML_HEADS)))),
            "wgr": b16(jnp.pad(w_gate.T, ((0, 16 - 2 * ML_HEADS), (0, 0)))),
            "bgc": jnp.pad(b_gate, (0, 128 - 2 * ML_HEADS)).reshape(1, 128).astype(F32),
            "bgr": jnp.pad(b_gate, (0, 16 - 2 * ML_HEADS)).reshape(16, 1).astype(F32),
            "cw": mlstm_conv_w[l].astype(F32), "cb": row(mlstm_conv_b[l]),
            "wdq_t": b16(dq.T), "wiq_t": b16(iq.T), "wiw_t": b16(jnp.pad(iw.T, ((0, 16 - IDX_HEADS), (0, 0)))),
            "wdc": b16(dc), "kvg": row(dsa_kv_norm_g[l]),
            "wik": b16(jnp.pad(ik, ((0, 0), (0, 128 - IDX_DIM)))), "ikg": row(idx_k_norm_g[l]),
            "headg": row(mlstm_head_norm_g[l]),
            "wuv_t": b16(jnp.swapaxes(dsa_w_uv[l], 1, 2)),
            "woa": w_out[l][:ML_WIDTH], "wob": w_out[l][ML_WIDTH:],
            "xg": row(xattn_norm_g[l]), "memg": row(mem_norm_g[l]),
            "wq": xattn_w_q[l], "wkv": xattn_w_kv[l], "wxo": xattn_w_o[l],
        }
        fg = row(final_norm_g) if l == depth - 1 else None
        h = _layer(h, mem2d, p, batch, seq, mem_len, fg)
    return h.reshape(batch, seq, d)
```
